```python
import jax, jax.numpy as jnp
from jax import lax
import numpy as np

D_MODEL = 2048
BATCH = 16
SEQ = 2048
DEPTH = 4
DEC_BATCH = 8
DEC_SEQ = 2048
PAST_LEN = 128

N_MIXERS = 2
N_HEADS = 16
N_KV_HEADS = 4
HEAD_DIM = D_MODEL // N_HEADS
Q_PER_KV = N_HEADS // N_KV_HEADS
ROPE_THETA = 10000.0
GRID_W = 64
Q_BLOCK = 128
N_META = 16
POOL_WINDOWS = (2, 4, 8, 16)
N_POOL_GROUPS = len(POOL_WINDOWS)
POOL_GROUP_DIM = D_MODEL // N_POOL_GROUPS
D_FF = 4 * D_MODEL
N_ATTN_LAYERS = (DEPTH + 1) // 2
N_POOL_LAYERS = DEPTH // 2
EPS = 1e-6

kernel_name = "hybrid_attn_pool_encoder"


def rmsnorm(x, gain):
    xf = x.astype(jnp.float32)
    xf = xf * lax.rsqrt(jnp.mean(xf * xf, axis=-1, keepdims=True) + EPS)
    return (xf * gain.astype(jnp.float32)).astype(x.dtype)


def head_rmsnorm_f32(x, gain):
    xf = x.astype(jnp.float32)
    xf = xf * lax.rsqrt(jnp.mean(xf * xf, axis=-1, keepdims=True) + EPS)
    return xf * gain.astype(jnp.float32)


def axial_rope_tables(n_tokens):
    rows = n_tokens // GRID_W
    t = np.arange(rows * GRID_W)
    r = (t // GRID_W).astype(np.float32)
    c = (t % GRID_W).astype(np.float32)
    axis_dim = HEAD_DIM // 2
    inv_freq = (ROPE_THETA ** (-np.arange(0, axis_dim, 2, dtype=np.float32) / axis_dim)).astype(np.float32)
    ang = np.concatenate([r[:, None] * inv_freq[None], c[:, None] * inv_freq[None]], axis=-1)
    ang = np.concatenate([np.zeros((N_META, HEAD_DIM // 2), np.float32), ang], axis=0)
    return jnp.asarray(np.cos(ang), jnp.float32), jnp.asarray(np.sin(ang), jnp.float32)


def apply_rope(x, cos, sin):
    B, L, H, _ = x.shape
    xp = x.reshape(B, L, H, HEAD_DIM // 2, 2)
    x0, x1 = xp[..., 0], xp[..., 1]
    c = cos[None, :, None, :]
    s = sin[None, :, None, :]
    out = jnp.stack([x0 * c - x1 * s, x0 * s + x1 * c], axis=-1)
    return out.reshape(B, L, H, HEAD_DIM)


def attention_mixer(h, w_qkv, q_gain, k_gain, w_o, cos, sin):
    B, L, _ = h.shape
    qkv = h @ w_qkv
    q, k, v = jnp.split(qkv, [N_HEADS * HEAD_DIM, (N_HEADS + N_KV_HEADS) * HEAD_DIM], axis=-1)
    q = q.reshape(B, L, N_HEADS, HEAD_DIM)
    k = k.reshape(B, L, N_KV_HEADS, HEAD_DIM)
    v = v.reshape(B, L, N_KV_HEADS, HEAD_DIM)
    q = (apply_rope(head_rmsnorm_f32(q, q_gain), cos, sin) * (HEAD_DIM ** -0.5)).astype(h.dtype)
    k = apply_rope(head_rmsnorm_f32(k, k_gain), cos, sin).astype(h.dtype)
    q = q.reshape(B, L, N_KV_HEADS, Q_PER_KV, HEAD_DIM)

    def attend(qb):
        s = jnp.einsum('bqkgd,bskd->bkgqs', qb, k, preferred_element_type=jnp.float32)
        p = jax.nn.softmax(s, axis=-1).astype(v.dtype)
        return jnp.einsum('bkgqs,bskd->bqkgd', p, v)

    o_meta = attend(q[:, :N_META])
    n_real = L - N_META
    n_blk = n_real // Q_BLOCK
    q_real = q[:, N_META:].reshape(B, n_blk, Q_BLOCK, N_KV_HEADS, Q_PER_KV, HEAD_DIM)
    q_real = q_real.transpose(1, 0, 2, 3, 4, 5)
    o_real = lax.map(attend, q_real).transpose(1, 0, 2, 3, 4, 5)
    o_real = o_real.reshape(B, n_real, N_KV_HEADS, Q_PER_KV, HEAD_DIM)
    o = jnp.concatenate([o_meta, o_real], axis=1).reshape(B, L, N_HEADS * HEAD_DIM)
    return o @ w_o


def pool_mixer(h, w_pool, pool_scale):
    B, L, _ = h.shape
    hg = h.reshape(B, L, N_POOL_GROUPS, POOL_GROUP_DIM)
    csum = jnp.cumsum(hg.astype(jnp.float32), axis=1)
    csum = jnp.concatenate([jnp.zeros((B, 1, N_POOL_GROUPS, POOL_GROUP_DIM), jnp.float32), csum], axis=1)
    t = np.arange(L)
    pooled = []
    for g, w in enumerate(POOL_WINDOWS):
        lo = np.clip(t - w // 2, 0, L)
        hi = np.clip(t - w // 2 + w, 0, L)
        cnt = (hi - lo).astype(np.float32)
        cg = csum[:, :, g]
        window_sum = jnp.take(cg, hi, axis=1) - jnp.take(cg, lo, axis=1)
        pooled.append(window_sum / cnt[None, :, None])
    pooled = jnp.stack(pooled, axis=2)
    mixed = (pooled - hg.astype(jnp.float32)).astype(h.dtype)
    y = jnp.einsum('blgc,gcd->blgd', mixed, w_pool).reshape(B, L, D_MODEL)
    return y * pool_scale


def sqrelu_mlp(h, w_up, w_down):
    return jnp.square(jax.nn.relu(h @ w_up)) @ w_down


def trunk(x, meta_tokens, attn_norm, w_qkv, q_norm, k_norm, w_o,
          pool_norm, w_pool, pool_scale, mlp_norm, w_up, w_down, final_norm):
    B, N, _ = x.shape
    cos, sin = axial_rope_tables(N)
    meta = jnp.broadcast_to(meta_tokens[None].astype(x.dtype), (B, N_META, D_MODEL))
    h = jnp.concatenate([meta, x], axis=1)
    for i in range(DEPTH):
        j = i // N_MIXERS
        if i % N_MIXERS == 0:
            h = h + attention_mixer(rmsnorm(h, attn_norm[j]), w_qkv[j], q_norm[j], k_norm[j], w_o[j], cos, sin)
        else:
            h = h + pool_mixer(rmsnorm(h, pool_norm[j]), w_pool[j], pool_scale[j])
        h = h + sqrelu_mlp(rmsnorm(h, mlp_norm[i]), w_up[i], w_down[i])
    return rmsnorm(h, final_norm)[:, N_META:]


def setup_inputs(seed: int = 0) -> dict:
    key = jax.random.key(seed)
    ks = jax.random.split(key, 16)
    f32 = jnp.float32
    qkv_out = (N_HEADS + 2 * N_KV_HEADS) * HEAD_DIM
    nrm = lambda k, shape, scale: jax.random.normal(k, shape, f32) * scale
    return {
        "x_prompt": nrm(ks[0], (BATCH, SEQ, D_MODEL), 1.0),
        "x_sample": nrm(ks[1], (DEC_BATCH, DEC_SEQ, D_MODEL), 1.0),
        "meta_tokens": nrm(ks[2], (N_META, D_MODEL), 1.0),
        "attn_norm": 1.0 + nrm(ks[3], (N_ATTN_LAYERS, D_MODEL), 0.02),
        "w_qkv": nrm(ks[4], (N_ATTN_LAYERS, D_MODEL, qkv_out), D_MODEL ** -0.5),
        "q_norm": 1.0 + nrm(ks[5], (N_ATTN_LAYERS, HEAD_DIM), 0.02),
        "k_norm": 1.0 + nrm(ks[6], (N_ATTN_LAYERS, HEAD_DIM), 0.02),
        "w_o": nrm(ks[7], (N_ATTN_LAYERS, N_HEADS * HEAD_DIM, D_MODEL), (N_HEADS * HEAD_DIM) ** -0.5),
        "pool_norm": 1.0 + nrm(ks[8], (N_POOL_LAYERS, D_MODEL), 0.02),
        "w_pool": nrm(ks[9], (N_POOL_LAYERS, N_POOL_GROUPS, POOL_GROUP_DIM, POOL_GROUP_DIM), POOL_GROUP_DIM ** -0.5),
        "pool_scale": 1.0 + nrm(ks[10], (N_POOL_LAYERS, D_MODEL), 0.02),
        "mlp_norm": 1.0 + nrm(ks[11], (DEPTH, D_MODEL), 0.02),
        "w_up": nrm(ks[12], (DEPTH, D_MODEL, D_FF), D_MODEL ** -0.5),
        "w_down": nrm(ks[13], (DEPTH, D_FF, D_MODEL), D_FF ** -0.5),
        "final_norm": 1.0 + nrm(ks[14], (D_MODEL,), 0.02),
    }


def reference(x_prompt, x_sample, meta_tokens, attn_norm, w_qkv, q_norm, k_norm, w_o,
              pool_norm, w_pool, pool_scale, mlp_norm, w_up, w_down, final_norm):
    y_prompt = trunk(x_prompt, meta_tokens, attn_norm, w_qkv, q_norm, k_norm, w_o,
                     pool_norm, w_pool, pool_scale, mlp_norm, w_up, w_down, final_norm)
    y_sample = trunk(x_sample, meta_tokens, attn_norm, w_qkv, q_norm, k_norm, w_o,
                     pool_norm, w_pool, pool_scale, mlp_norm, w_up, w_down, final_norm)
    return (y_prompt, y_sample)
```

```python
import functools

import numpy as np
import jax
import jax.numpy as jnp
from jax import lax
from jax.experimental import pallas as pl
from jax.experimental.pallas import tpu as pltpu

N_HEADS = 16
N_KV_HEADS = 4
Q_PER_KV = N_HEADS // N_KV_HEADS
HEAD_DIM = 128
ROPE_THETA = 10000.0
GRID_W = 64
N_META = 16
POOL_WINDOWS = (2, 4, 8, 16)
EPS = 1e-6

LANES = 128
F32_SUBLANES = 8
HALO = max(POOL_WINDOWS) // 2
ROW_TILE = 512
FF_TILE = 1024
Q_TILE = 512
MXU_COLS = 256
VMEM_LIMIT_BYTES = 56 * 1024 * 1024

assert HALO == F32_SUBLANES and N_META == 2 * HALO and HEAD_DIM == LANES

BF16 = jnp.bfloat16
F32 = jnp.float32


def _params(*semantics):
    return pltpu.CompilerParams(dimension_semantics=semantics, vmem_limit_bytes=VMEM_LIMIT_BYTES)


def _rmsnorm(x, gain):
    ms = jnp.mean(x * x, axis=-1, keepdims=True)
    return x * lax.rsqrt(ms + EPS) * gain


def _resident(shape):
    return pl.BlockSpec(shape, lambda *_: (0,) * len(shape), pipeline_mode=pl.Buffered(1))


def _rope_tables(seq):
    t = np.arange((seq // GRID_W) * GRID_W)
    r = (t // GRID_W).astype(np.float32)
    c = (t % GRID_W).astype(np.float32)
    axis_dim = HEAD_DIM // 2
    inv_freq = (ROPE_THETA ** (-np.arange(0, axis_dim, 2, dtype=np.float32) / axis_dim)).astype(np.float32)
    ang = np.concatenate([r[:, None] * inv_freq[None], c[:, None] * inv_freq[None]], axis=-1)
    cos = np.repeat(np.cos(ang), 2, axis=-1).astype(np.float32)
    sin = np.repeat(np.sin(ang), 2, axis=-1).astype(np.float32)
    even = (np.arange(HEAD_DIM) % 2) == 0
    sin_next = np.where(even, -sin, 0.0).astype(np.float32)
    sin_prev = np.where(even, 0.0, sin).astype(np.float32)
    ident = np.ones((ROW_TILE, HEAD_DIM), np.float32)
    zero = np.zeros((ROW_TILE, HEAD_DIM), np.float32)
    return (jnp.asarray(np.concatenate([cos, ident])),
            jnp.asarray(np.concatenate([sin_next, zero])),
            jnp.asarray(np.concatenate([sin_prev, zero])))


def _qkv_kernel(h_ref, gain_ref, w_ref, qg_ref, kg_ref, cos_ref, sn_ref, sp_ref, q_ref, k_ref, v_ref):
    xn = _rmsnorm(h_ref[...], gain_ref[...]).astype(BF16)
    cos, sin_next, sin_prev = cos_ref[...], sn_ref[...], sp_ref[...]
    nq, nk = q_ref.shape[1], k_ref.shape[1]

    def norm_rope(y, gain):
        yn = _rmsnorm(y, gain)
        return (yn * cos + pltpu.roll(yn, HEAD_DIM - 1, 1) * sin_next + pltpu.roll(yn, 1, 1) * sin_prev)

    for c0 in range(0, nq + 2 * nk, MXU_COLS):
        y = jnp.dot(xn, w_ref[:, c0:c0 + MXU_COLS], preferred_element_type=F32)
        for s0 in range(0, MXU_COLS, HEAD_DIM):
            col, yh = c0 + s0, y[:, s0:s0 + HEAD_DIM]
            if col < nq:
                q_ref[:, col:col + HEAD_DIM] = norm_rope(yh, qg_ref[...]).astype(BF16)
            elif col < nq + nk:
                k_ref[:, col - nq:col - nq + HEAD_DIM] = norm_rope(yh, kg_ref[...]).astype(BF16)
            else:
                v_ref[:, col - nq - nk:col - nq - nk + HEAD_DIM] = yh.astype(BF16)


def _qkv_call(h, gain, w_qkv, q_gain, k_gain, tables, seq, n_real):
    ntok, d = h.shape
    nq, nk = N_HEADS * HEAD_DIM, N_KV_HEADS * HEAD_DIM
    tiles_per_seq, n_real_tiles = seq // ROW_TILE, n_real // ROW_TILE
    row = lambda width: pl.BlockSpec((ROW_TILE, width), lambda i: (i, 0))
    table = pl.BlockSpec((ROW_TILE, HEAD_DIM),
                         lambda i: (jnp.where(i < n_real_tiles, i % tiles_per_seq, tiles_per_seq), 0))
    vec = lambda width: pl.BlockSpec((1, width), lambda i: (0, 0))
    return pl.pallas_call(
        _qkv_kernel,
        grid=(ntok // ROW_TILE,),
        in_specs=[row(d), vec(d), _resident((d, nq + 2 * nk)), vec(HEAD_DIM), vec(HEAD_DIM), table, table, table],
        out_specs=[row(nq), row(nk), row(nk)],
        out_shape=[jax.ShapeDtypeStruct((ntok, nq), BF16),
                   jax.ShapeDtypeStruct((ntok, nk), BF16),
                   jax.ShapeDtypeStruct((ntok, nk), BF16)],
        compiler_params=_params("parallel"),
        name="qkv_proj",
    )(h, gain, w_qkv, q_gain, k_gain, *tables)


def _attn_kernel(q_ref, qm_ref, k_ref, km_ref, v_ref, vm_ref, zeros_ref, o_ref, om_ref, kpad_ref, vpad_ref):
    del zeros_ref
    first_q_tile = pl.program_id(2) == 0

    @pl.when(first_q_tile)
    def _():
        pad = jnp.zeros((LANES - N_META, HEAD_DIM), BF16)
        kpad_ref[0:N_META, :] = km_ref[...]
        kpad_ref[N_META:, :] = pad
        vpad_ref[0:N_META, :] = vm_ref[...]
        vpad_ref[N_META:, :] = pad

    def attend(q):
        outs = []
        for hq in range(Q_PER_KV):
            qh = q[:, hq * HEAD_DIM:(hq + 1) * HEAD_DIM]
            contract_last = (((1,), (1,)), ((), ()))
            s = lax.dot_general(qh, k_ref[...], contract_last, preferred_element_type=F32)
            sm = lax.dot_general(qh, kpad_ref[...], contract_last, preferred_element_type=F32)
            lane = lax.broadcasted_iota(jnp.int32, sm.shape, 1)
            sm = jnp.where(lane < N_META, sm, -jnp.inf)
            m = jnp.maximum(jnp.max(s, axis=-1, keepdims=True), jnp.max(sm, axis=-1, keepdims=True))
            p, pm = jnp.exp(s - m), jnp.exp(sm - m)
            denom = jnp.sum(p, axis=-1, keepdims=True) + jnp.sum(pm, axis=-1, keepdims=True)
            o = (jnp.dot(p.astype(BF16), v_ref[...], preferred_element_type=F32)
                 + jnp.dot(pm.astype(BF16), vpad_ref[...], preferred_element_type=F32))
            outs.append(o / denom)
        return jnp.concatenate(outs, axis=-1).astype(BF16)

    o_ref[...] = attend(q_ref[...])

    @pl.when(first_q_tile)
    def _():
        om_ref[...] = attend(qm_ref[...])


def _attn_call(q, k, v, nseq, seq, n_real):
    gw = Q_PER_KV * HEAD_DIM
    q_tiles = seq // Q_TILE
    meta0 = n_real // N_META
    q_spec = pl.BlockSpec((Q_TILE, gw), lambda b, g, i: (b * q_tiles + i, g))
    qm_spec = pl.BlockSpec((N_META, gw), lambda b, g, i: (meta0 + b, g))
    kv_spec = pl.BlockSpec((seq, HEAD_DIM), lambda b, g, i: (b, g))
    kvm_spec = pl.BlockSpec((N_META, HEAD_DIM), lambda b, g, i: (meta0 + b, g))
    om_spec = pl.BlockSpec((N_META, gw), lambda b, g, i: (b, g))
    zeros = jnp.zeros((ROW_TILE, N_HEADS * HEAD_DIM), BF16)
    return pl.pallas_call(
        _attn_kernel,
        grid=(nseq, N_KV_HEADS, q_tiles),
        in_specs=[q_spec, qm_spec, kv_spec, kvm_spec, kv_spec, kvm_spec, pl.BlockSpec(memory_space=pl.ANY)],
        out_specs=[q_spec, om_spec],
        out_shape=[jax.ShapeDtypeStruct((n_real, N_HEADS * HEAD_DIM), BF16),
                   jax.ShapeDtypeStruct(zeros.shape, BF16)],
        scratch_shapes=[pltpu.VMEM((LANES, HEAD_DIM), BF16), pltpu.VMEM((LANES, HEAD_DIM), BF16)],
        input_output_aliases={6: 1},
        compiler_params=_params("parallel", "parallel", "arbitrary"),
        name="attention",
    )(q, q, k, k, v, v, zeros)


def _oproj_kernel(h_ref, o_ref, om_ref, w_ref, out_ref):
    is_meta_tile = pl.program_id(0) == pl.num_programs(0) - 1
    o = jnp.where(is_meta_tile, om_ref[...], o_ref[...])
    out_ref[...] = h_ref[...] + jnp.dot(o, w_ref[...], preferred_element_type=F32)


def _oproj_call(h, o_real, o_meta, w_o):
    ntok, d = h.shape
    n_real_tiles = o_real.shape[0] // ROW_TILE
    row = pl.BlockSpec((ROW_TILE, d), lambda i: (i, 0))
    o_spec = pl.BlockSpec((ROW_TILE, o_real.shape[1]), lambda i: (jnp.minimum(i, n_real_tiles - 1), 0))
    om_spec = pl.BlockSpec((ROW_TILE, o_real.shape[1]), lambda i: (0, 0))
    return pl.pallas_call(
        _oproj_kernel,
        grid=(ntok // ROW_TILE,),
        in_specs=[row, o_spec, om_spec, _resident(w_o.shape)],
        out_specs=row,
        out_shape=jax.ShapeDtypeStruct(h.shape, F32),
        compiler_params=_params("parallel"),
        name="attn_out_proj",
    )(h, o_real, o_meta, w_o)


def _mlp_kernel(h_ref, gain_ref, wu_ref, wd_ref, fgain_ref, out_ref, xn_ref, *, final_norm):
    k = pl.program_id(1)

    @pl.when(k == 0)
    def _():
        x = h_ref[...]
        xn_ref[...] = _rmsnorm(x, gain_ref[...]).astype(BF16)
        out_ref[...] = x

    u = jnp.dot(xn_ref[...], wu_ref[...], preferred_element_type=F32)
    a = jnp.square(jnp.maximum(u, 0.0)).astype(BF16)
    out_ref[...] += jnp.dot(a, wd_ref[...], preferred_element_type=F32)

    if final_norm:
        @pl.when(k == pl.num_programs(1) - 1)
        def _():
            out_ref[...] = _rmsnorm(out_ref[...], fgain_ref[...])


def _mlp_call(h, gain, w_up, w_down, final_gain, first_tile, n_tiles, final_norm):
    d, dff = w_up.shape
    vec = pl.BlockSpec((1, d), lambda i, k: (0, 0))
    return pl.pallas_call(
        functools.partial(_mlp_kernel, final_norm=final_norm),
        grid=(n_tiles, dff // FF_TILE),
        in_specs=[pl.BlockSpec((ROW_TILE, d), lambda i, k: (first_tile + i, 0)), vec,
                  pl.BlockSpec((d, FF_TILE), lambda i, k: (0, k)),
                  pl.BlockSpec((FF_TILE, d), lambda i, k: (k, 0)), vec],
        out_specs=pl.BlockSpec((ROW_TILE, d), lambda i, k: (i, 0)),
        out_shape=jax.ShapeDtypeStruct((n_tiles * ROW_TILE, d), F32),
        scratch_shapes=[pltpu.VMEM((ROW_TILE, d), BF16)],
        compiler_params=_params("parallel", "arbitrary"),
        name="mlp_final" if final_norm else "mlp",
    )(h, gain, w_up, w_down, final_gain)


def _pool_rows(x, xe_ref, first_pos, seq_len, w_ref, scale_ref):
    n = x.shape[0]
    gdim = w_ref.shape[1]
    pos = first_pos + lax.broadcasted_iota(jnp.int32, (n, LANES), 0)
    outs = []
    for g, w in enumerate(POOL_WINDOWS):
        cols = slice(g * gdim, (g + 1) * gdim)
        acc = xe_ref[HALO - w // 2:HALO - w // 2 + n, cols]
        for dlt in range(1 - w // 2, w // 2):
            acc = acc + xe_ref[HALO + dlt:HALO + dlt + n, cols]
        cnt = (jnp.minimum(pos + w // 2, seq_len) - jnp.maximum(pos - w // 2, 0)).astype(F32)
        inv = jnp.tile(1.0 / cnt, (1, gdim // LANES))
        mixed = (acc * inv - xe_ref[HALO:HALO + n, cols]).astype(BF16)
        y = jnp.dot(mixed, w_ref[g], preferred_element_type=F32)
        outs.append(x[:, cols] + y * scale_ref[:, cols])
    return jnp.concatenate(outs, axis=-1)


def _pool_kernel(cur_ref, prev_ref, next_ref, gain_ref, w_ref, scale_ref, out_ref, xe_ref, *, seq):
    j = pl.program_id(1)
    gain = gain_ref[...]
    x = cur_ref[...]
    n = x.shape[0]
    xe_ref[0:HALO, :] = _rmsnorm(prev_ref[...], gain)
    xe_ref[HALO:HALO + n, :] = _rmsnorm(x, gain)
    nxt = _rmsnorm(next_ref[...], gain)
    xe_ref[HALO + n:, :] = jnp.where(j == pl.num_programs(1) - 1, 0.0, nxt)
    out_ref[...] = _pool_rows(x, xe_ref, N_META + j * n, N_META + seq, w_ref, scale_ref)


def _pool_meta_kernel(h_any, cur_ref, next_ref, gain_ref, w_ref, scale_ref, out_ref, xe_ref, *, nseq, seq):
    del h_any
    b = pl.program_id(0)

    @pl.when(b < nseq)
    def _():
        gain = gain_ref[...]
        x = cur_ref[...]
        xe_ref[0:HALO, :] = jnp.zeros((HALO, x.shape[1]), F32)
        xe_ref[HALO:HALO + N_META, :] = _rmsnorm(x, gain)
        xe_ref[HALO + N_META:, :] = _rmsnorm(next_ref[...], gain)
        out_ref[...] = _pool_rows(x, xe_ref, 0, N_META + seq, w_ref, scale_ref)

    @pl.when(b >= nseq)
    def _():
        out_ref[...] = jnp.zeros(out_ref.shape, F32)


def _pool_call(h, gain, w_pool, scale, nseq, seq, n_real, with_meta):
    ntok, d = h.shape
    tiles = seq // ROW_TILE
    blk = ROW_TILE // HALO
    meta_blk0 = n_real // HALO
    vec = lambda nd: pl.BlockSpec((1, d), lambda *_: (0, 0))
    prev_spec = pl.BlockSpec((HALO, d), lambda b, j: (jnp.where(j == 0, meta_blk0 + 2 * b + 1, (b * tiles + j) * blk - 1), 0))
    next_spec = pl.BlockSpec((HALO, d), lambda b, j: (jnp.where(j == tiles - 1, 0, (b * tiles + j + 1) * blk), 0))
    cur_spec = pl.BlockSpec((ROW_TILE, d), lambda b, j: (b * tiles + j, 0))
    out_rows = ntok if with_meta else n_real
    out = pl.pallas_call(
        functools.partial(_pool_kernel, seq=seq),
        grid=(nseq, tiles),
        in_specs=[cur_spec, prev_spec, next_spec, vec(2), _resident(w_pool.shape), vec(2)],
        out_specs=cur_spec,
        out_shape=jax.ShapeDtypeStruct((out_rows, d), F32),
        scratch_shapes=[pltpu.VMEM((ROW_TILE + 2 * HALO, d), F32)],
        compiler_params=_params("parallel", "parallel"),
        name="pool_mixer",
    )(h, h, h, gain, w_pool, scale)
    if not with_meta:
        return out
    meta0 = n_real // N_META
    n_tail = (ntok - n_real) // N_META
    return pl.pallas_call(
        functools.partial(_pool_meta_kernel, nseq=nseq, seq=seq),
        grid=(n_tail,),
        in_specs=[pl.BlockSpec(memory_space=pl.ANY),
                  pl.BlockSpec((N_META, d), lambda b: (meta0 + jnp.minimum(b, nseq - 1), 0)),
                  pl.BlockSpec((HALO, d), lambda b: (jnp.minimum(b, nseq - 1) * tiles * blk, 0)),
                  vec(1), _resident(w_pool.shape), vec(1)],
        out_specs=pl.BlockSpec((N_META, d), lambda b: (meta0 + b, 0)),
        out_shape=jax.ShapeDtypeStruct((ntok, d), F32),
        scratch_shapes=[pltpu.VMEM((N_META + 2 * HALO, d), F32)],
        input_output_aliases={0: 0},
        compiler_params=_params("arbitrary"),
        name="pool_mixer_meta",
    )(out, h, h, gain, w_pool, scale)


def kernel(x_prompt, x_sample, meta_tokens, attn_norm, w_qkv, q_norm, k_norm, w_o, pool_norm, w_pool, pool_scale,
           mlp_norm, w_up, w_down, final_norm):
    b1, seq, d = x_prompt.shape
    b2 = x_sample.shape[0]
    assert x_sample.shape[1:] == (seq, d) and seq % ROW_TILE == 0 and seq % GRID_W == 0
    nseq = b1 + b2
    n_real = nseq * seq
    assert nseq * N_META <= ROW_TILE
    depth = mlp_norm.shape[0]

    h = jnp.concatenate([
        x_prompt.reshape(b1 * seq, d), x_sample.reshape(b2 * seq, d),
        jnp.tile(meta_tokens.astype(F32), (nseq, 1)),
        jnp.zeros((ROW_TILE - nseq * N_META, d), F32)], axis=0)
    ntok = h.shape[0]

    tables = _rope_tables(seq)
    w_qkv, w_o, w_pool, w_up, w_down = (w.astype(BF16) for w in (w_qkv, w_o, w_pool, w_up, w_down))
    row2 = lambda a: a.reshape(1, -1).astype(F32)
    q_gain = q_norm.astype(F32) * (HEAD_DIM ** -0.5)

    outs = None
    for i in range(depth):
        j = i // 2
        last = i == depth - 1
        if i % 2 == 0:
            q, k, v = _qkv_call(h, row2(attn_norm[j]), w_qkv[j], row2(q_gain[j]), row2(k_norm[j]), tables, seq, n_real)
            o_real, o_meta = _attn_call(q, k, v, nseq, seq, n_real)
            h = _oproj_call(h, o_real, o_meta, w_o[j])
        else:
            h = _pool_call(h, row2(pool_norm[j]), w_pool[j], row2(pool_scale[j]), nseq, seq, n_real, with_meta=not last)
        if not last:
            h = _mlp_call(h, row2(mlp_norm[i]), w_up[i], w_down[i], row2(final_norm), 0, ntok // ROW_TILE, False)
        else:
            t1 = b1 * seq // ROW_TILE
            t2 = b2 * seq // ROW_TILE
            mlp = functools.partial(_mlp_call, h, row2(mlp_norm[i]), w_up[i], w_down[i], row2(final_norm))
            outs = (mlp(0, t1, True).reshape(b1, seq, d), mlp(t1, t2, True).reshape(b2, seq, d))
    return outs
```

```python
import functools
import math

import numpy as np
import jax
import jax.numpy as jnp
from jax import lax
from jax.experimental import pallas as pl
from jax.experimental.pallas import tpu as pltpu

N_HEADS = 16
N_KV_HEADS = 4
Q_PER_KV = N_HEADS // N_KV_HEADS
HEAD_DIM = 128
ROPE_THETA = 10000.0
GRID_W = 64
N_META = 16
POOL_WINDOWS = (2, 4, 8, 16)
EPS = 1e-6

LANES = 128
F32_SUBLANES = 8
HALO = max(POOL_WINDOWS) // 2
ROW_TILE = 512
FF_TILE = 1024
Q_TILE = 512
MXU_COLS = 256
VMEM_LIMIT_BYTES = 56 * 1024 * 1024

assert HALO == F32_SUBLANES and N_META == 2 * HALO and HEAD_DIM == LANES

BF16 = jnp.bfloat16
F32 = jnp.float32


def _params(*semantics):
    return pltpu.CompilerParams(dimension_semantics=semantics, vmem_limit_bytes=VMEM_LIMIT_BYTES)


def _rmsnorm(x, gain):
    ms = jnp.mean(x * x, axis=-1, keepdims=True)
    return x * lax.rsqrt(ms + EPS) * gain


def _layer_resident(w, layer):
    zeros = (0,) * (w.ndim - 1)
    return pl.BlockSpec((None,) + w.shape[1:], lambda *_: (layer,) + zeros, pipeline_mode=pl.Buffered(1))


def _vec_spec(width):
    return pl.BlockSpec((1, width), lambda *_: (0, 0))


def _row_part_specs(parts, d):
    specs, start = [], 0
    for _, n in parts:
        specs.append(pl.BlockSpec((ROW_TILE, d), functools.partial(
            lambda i, *_, start, n: (jnp.clip(i - start, 0, n - 1), 0), start=start, n=n)))
        start += n
    return specs


def _read_row_tile(refs, counts):
    i = pl.program_id(0)
    x = refs[0][...]
    start = 0
    for ref, n in zip(refs[1:], counts[:-1]):
        start += n
        x = jnp.where(i >= start, ref[...], x)
    return x


def _split_pairs(w, n_heads):
    lead = w.shape[:-1]
    return w.reshape(*lead, n_heads, HEAD_DIM // 2, 2).swapaxes(-1, -2).reshape(*lead, n_heads * HEAD_DIM)


def _rope_tables(seq):
    t = np.arange((seq // GRID_W) * GRID_W)
    r = (t // GRID_W).astype(np.float32)
    c = (t % GRID_W).astype(np.float32)
    axis_dim = HEAD_DIM // 2
    inv_freq = (ROPE_THETA ** (-np.arange(0, axis_dim, 2, dtype=np.float32) / axis_dim)).astype(np.float32)
    ang = np.concatenate([r[:, None] * inv_freq[None], c[:, None] * inv_freq[None]], axis=-1)
    cos, sin = np.cos(ang).astype(np.float32), np.sin(ang).astype(np.float32)
    cos = np.concatenate([cos, cos], axis=-1)
    sin = np.concatenate([-sin, sin], axis=-1)
    ident = np.ones((ROW_TILE, HEAD_DIM), np.float32)
    zero = np.zeros((ROW_TILE, HEAD_DIM), np.float32)
    return jnp.asarray(np.concatenate([cos, ident])), jnp.asarray(np.concatenate([sin, zero]))


def _qkv_kernel(*refs, counts):
    n = len(counts)
    gain_ref, w_ref, qg_ref, kg_ref, cos_ref, sin_ref, q_ref, k_ref, v_ref = refs[n:]
    xn = _rmsnorm(_read_row_tile(refs[:n], counts), gain_ref[...]).astype(BF16)
    cos, sin = cos_ref[...], sin_ref[...]
    nq, nk = q_ref.shape[1], k_ref.shape[1]

    def norm_rope(y, gain):
        yn = _rmsnorm(y, gain)
        return yn * cos + pltpu.roll(yn, HEAD_DIM // 2, 1) * sin

    for c0 in range(0, nq + 2 * nk, MXU_COLS):
        y = jnp.dot(xn, w_ref[:, c0:c0 + MXU_COLS], preferred_element_type=F32)
        for s0 in range(0, MXU_COLS, HEAD_DIM):
            col, yh = c0 + s0, y[:, s0:s0 + HEAD_DIM]
            if col < nq:
                q_ref[:, col:col + HEAD_DIM] = norm_rope(yh, qg_ref[...]).astype(BF16)
            elif col < nq + nk:
                k_ref[:, col - nq:col - nq + HEAD_DIM] = norm_rope(yh, kg_ref[...]).astype(BF16)
            else:
                v_ref[:, col - nq - nk:col - nq - nk + HEAD_DIM] = yh.astype(BF16)


def _qkv_call(parts, gain, w_qkv, layer, q_gain, k_gain, tables, seq, n_real):
    d = w_qkv.shape[1]
    counts = tuple(n for _, n in parts)
    ntok = sum(counts) * ROW_TILE
    nq, nk = N_HEADS * HEAD_DIM, N_KV_HEADS * HEAD_DIM
    tiles_per_seq, n_real_tiles = seq // ROW_TILE, n_real // ROW_TILE
    row = lambda width: pl.BlockSpec((ROW_TILE, width), lambda i: (i, 0))
    table = pl.BlockSpec((ROW_TILE, HEAD_DIM),
                         lambda i: (jnp.where(i < n_real_tiles, i % tiles_per_seq, tiles_per_seq), 0))
    return pl.pallas_call(
        functools.partial(_qkv_kernel, counts=counts),
        grid=(ntok // ROW_TILE,),
        in_specs=_row_part_specs(parts, d) + [_vec_spec(d), _layer_resident(w_qkv, layer), _vec_spec(HEAD_DIM),
                                              _vec_spec(HEAD_DIM), table, table],
        out_specs=[row(nq), row(nk), row(nk)],
        out_shape=[jax.ShapeDtypeStruct((ntok, nq), BF16),
                   jax.ShapeDtypeStruct((ntok, nk), BF16),
                   jax.ShapeDtypeStruct((ntok, nk), BF16)],
        compiler_params=_params("parallel"),
        name="qkv_proj",
    )(*[a for a, _ in parts], gain, w_qkv, q_gain, k_gain, *tables)


def _attn_kernel(q_ref, qm_ref, k_ref, km_ref, v_ref, vm_ref, zeros_ref, o_ref, om_ref, kpad_ref, vt_ref, vtpad_ref):
    del zeros_ref
    first_q_tile = pl.program_id(2) == 0

    @pl.when(first_q_tile)
    def _():
        pad = jnp.zeros((LANES - N_META, HEAD_DIM), BF16)
        kpad_ref[0:N_META, :] = km_ref[...]
        kpad_ref[N_META:, :] = pad
        vt_ref[...] = v_ref[...].astype(F32).T.astype(BF16)
        vtpad_ref[...] = jnp.concatenate([vm_ref[...], pad], axis=0).astype(F32).T.astype(BF16)

    def scores(qh):
        contract_last = (((1,), (1,)), ((), ()))
        st = lax.dot_general(k_ref[...], qh, contract_last, preferred_element_type=F32)
        stm = lax.dot_general(kpad_ref[...], qh, contract_last, preferred_element_type=F32)
        key = lax.broadcasted_iota(jnp.int32, stm.shape, 0)
        return st, jnp.where(key < N_META, stm, -jnp.inf)

    def softmax_pv(st, stm):
        m = jnp.maximum(jnp.max(st, axis=0, keepdims=True), jnp.max(stm, axis=0, keepdims=True))
        p, pm = jnp.exp2(st - m), jnp.exp2(stm - m)
        denom = jnp.sum(p, axis=0, keepdims=True) + jnp.sum(pm, axis=0, keepdims=True)
        ot = (jnp.dot(vt_ref[...], p.astype(BF16), preferred_element_type=F32)
              + jnp.dot(vtpad_ref[...], pm.astype(BF16), preferred_element_type=F32))
        return (ot * (1.0 / denom)).T

    def attend(qh):
        return softmax_pv(*scores(qh))

    head_cols = [slice(hq * HEAD_DIM, (hq + 1) * HEAD_DIM) for hq in range(Q_PER_KV)]
    head_scores = [scores(q_ref[:, cols]) for cols in head_cols]
    for cols, s in zip(head_cols, head_scores):
        o_ref[:, cols] = softmax_pv(*s).astype(BF16)

    @pl.when(first_q_tile)
    def _():
        qm = [qm_ref[:, hq * HEAD_DIM:(hq + 1) * HEAD_DIM] for hq in range(Q_PER_KV)]
        qm.append(jnp.zeros((LANES - Q_PER_KV * N_META, HEAD_DIM), BF16))
        om = attend(jnp.concatenate(qm, axis=0))
        for hq in range(Q_PER_KV):
            om_ref[:, hq * HEAD_DIM:(hq + 1) * HEAD_DIM] = om[hq * N_META:(hq + 1) * N_META].astype(BF16)


def _attn_call(q, k, v, nseq, seq, n_real):
    gw = Q_PER_KV * HEAD_DIM
    q_tiles = seq // Q_TILE
    meta0 = n_real // N_META
    q_spec = pl.BlockSpec((Q_TILE, gw), lambda b, g, i: (b * q_tiles + i, g))
    qm_spec = pl.BlockSpec((N_META, gw), lambda b, g, i: (meta0 + b, g))
    kv_spec = pl.BlockSpec((seq, HEAD_DIM), lambda b, g, i: (b, g))
    kvm_spec = pl.BlockSpec((N_META, HEAD_DIM), lambda b, g, i: (meta0 + b, g))
    om_spec = pl.BlockSpec((N_META, gw), lambda b, g, i: (b, g))
    zeros = jnp.zeros((ROW_TILE, N_HEADS * HEAD_DIM), BF16)
    return pl.pallas_call(
        _attn_kernel,
        grid=(nseq, N_KV_HEADS, q_tiles),
        in_specs=[q_spec, qm_spec, kv_spec, kvm_spec, kv_spec, kvm_spec, pl.BlockSpec(memory_space=pl.ANY)],
        out_specs=[q_spec, om_spec],
        out_shape=[jax.ShapeDtypeStruct((n_real, N_HEADS * HEAD_DIM), BF16),
                   jax.ShapeDtypeStruct(zeros.shape, BF16)],
        scratch_shapes=[pltpu.VMEM((LANES, HEAD_DIM), BF16), pltpu.VMEM((HEAD_DIM, seq), BF16),
                        pltpu.VMEM((HEAD_DIM, LANES), BF16)],
        input_output_aliases={6: 1},
        compiler_params=_params("parallel", "parallel", "arbitrary"),
        name="attention",
    )(q, q, k, k, v, v, zeros)


def _oproj_kernel(*refs, counts):
    n = len(counts)
    o_ref, om_ref, w_ref, out_ref = refs[n:]
    is_meta_tile = pl.program_id(0) == pl.num_programs(0) - 1
    o = jnp.where(is_meta_tile, om_ref[...], o_ref[...])
    out_ref[...] = _read_row_tile(refs[:n], counts) + jnp.dot(o, w_ref[...], preferred_element_type=F32)


def _oproj_call(parts, o_real, o_meta, w_o, layer):
    d = w_o.shape[2]
    counts = tuple(n for _, n in parts)
    n_tiles = sum(counts)
    n_real_tiles = o_real.shape[0] // ROW_TILE
    o_spec = pl.BlockSpec((ROW_TILE, o_real.shape[1]), lambda i: (jnp.minimum(i, n_real_tiles - 1), 0))
    om_spec = pl.BlockSpec((ROW_TILE, o_real.shape[1]), lambda i: (0, 0))
    return pl.pallas_call(
        functools.partial(_oproj_kernel, counts=counts),
        grid=(n_tiles,),
        in_specs=_row_part_specs(parts, d) + [o_spec, om_spec, _layer_resident(w_o, layer)],
        out_specs=pl.BlockSpec((ROW_TILE, d), lambda i: (i, 0)),
        out_shape=jax.ShapeDtypeStruct((n_tiles * ROW_TILE, d), F32),
        compiler_params=_params("parallel"),
        name="attn_out_proj",
    )(*[a for a, _ in parts], o_real, o_meta, w_o)


def _mlp_kernel(h_ref, gain_ref, wu_ref, wd_ref, fgain_ref, out_ref, xn_ref, *, final_norm):
    k = pl.program_id(1)

    @pl.when(k == 0)
    def _():
        x = h_ref[...]
        xn_ref[...] = _rmsnorm(x, gain_ref[...]).astype(BF16)
        out_ref[...] = x

    u = jnp.dot(xn_ref[...], wu_ref[...], preferred_element_type=F32)
    a = jnp.square(jnp.maximum(u, 0.0)).astype(BF16)
    out_ref[...] += jnp.dot(a, wd_ref[...], preferred_element_type=F32)

    if final_norm:
        @pl.when(k == pl.num_programs(1) - 1)
        def _():
            out_ref[...] = _rmsnorm(out_ref[...], fgain_ref[...])


def _mlp_call(h, gain, w_up, w_down, layer, final_gain, first_tile, n_tiles, final_norm):
    _, d, dff = w_up.shape
    return pl.pallas_call(
        functools.partial(_mlp_kernel, final_norm=final_norm),
        grid=(n_tiles, dff // FF_TILE),
        in_specs=[pl.BlockSpec((ROW_TILE, d), lambda i, k: (first_tile + i, 0)), _vec_spec(d),
                  pl.BlockSpec((None, d, FF_TILE), lambda i, k: (layer, 0, k)),
                  pl.BlockSpec((None, FF_TILE, d), lambda i, k: (layer, k, 0)), _vec_spec(d)],
        out_specs=pl.BlockSpec((ROW_TILE, d), lambda i, k: (i, 0)),
        out_shape=jax.ShapeDtypeStruct((n_tiles * ROW_TILE, d), F32),
        scratch_shapes=[pltpu.VMEM((ROW_TILE, d), BF16)],
        compiler_params=_params("parallel", "arbitrary"),
        name="mlp_final" if final_norm else "mlp",
    )(h, gain, w_up, w_down, final_gain)


def _pool_rows(x, xe_ref, first_pos, seq_len, w_ref, scale_ref):
    n = x.shape[0]
    gdim = w_ref.shape[1]
    pos = first_pos + lax.broadcasted_iota(jnp.int32, (n, LANES), 0)
    outs = []
    for g, w in enumerate(POOL_WINDOWS):
        cols = slice(g * gdim, (g + 1) * gdim)
        acc = xe_ref[HALO - w // 2:HALO - w // 2 + n, cols]
        for dlt in range(1 - w // 2, w // 2):
            acc = acc + xe_ref[HALO + dlt:HALO + dlt + n, cols]
        cnt = (jnp.minimum(pos + w // 2, seq_len) - jnp.maximum(pos - w // 2, 0)).astype(F32)
        inv = jnp.tile(1.0 / cnt, (1, gdim // LANES))
        mixed = (acc * inv - xe_ref[HALO:HALO + n, cols]).astype(BF16)
        y = jnp.dot(mixed, w_ref[g], preferred_element_type=F32)
        outs.append(x[:, cols] + y * scale_ref[:, cols])
    return jnp.concatenate(outs, axis=-1)


def _pool_kernel(*refs, seq, n_real_tiles, with_tail):
    cur_ref, prev_ref, next_ref, gain_ref, w_ref, scale_ref = refs[:6]
    out_ref, xe_ref = refs[-2:]
    i = pl.program_id(0)
    tiles = seq // ROW_TILE
    j = i % tiles

    @pl.when(i < n_real_tiles)
    def _():
        gain = gain_ref[...]
        x = cur_ref[...]
        n = x.shape[0]
        xe_ref[0:HALO, :] = _rmsnorm(prev_ref[...], gain)
        xe_ref[HALO:HALO + n, :] = _rmsnorm(x, gain)
        nxt = _rmsnorm(next_ref[...], gain)
        xe_ref[HALO + n:, :] = jnp.where(j == tiles - 1, 0.0, nxt)
        out_ref[...] = _pool_rows(x, xe_ref, N_META + j * n, N_META + seq, w_ref, scale_ref)

    if with_tail:
        @pl.when(i == n_real_tiles)
        def _():
            out_ref[...] = refs[6][...]


def _pool_meta_kernel(cur_ref, next_ref, gain_ref, w_ref, scale_ref, out_ref, xe_ref, *, nseq, seq):
    b = pl.program_id(0)

    @pl.when(b < nseq)
    def _():
        gain = gain_ref[...]
        x = cur_ref[...]
        xe_ref[0:HALO, :] = jnp.zeros((HALO, x.shape[1]), F32)
        xe_ref[HALO:HALO + N_META, :] = _rmsnorm(x, gain)
        xe_ref[HALO + N_META:, :] = _rmsnorm(next_ref[...], gain)
        out_ref[...] = _pool_rows(x, xe_ref, 0, N_META + seq, w_ref, scale_ref)

    @pl.when(b >= nseq)
    def _():
        out_ref[...] = jnp.zeros(out_ref.shape, F32)


def _pool_call(h, gain, w_pool, layer, scale, nseq, seq, n_real, with_meta):
    d = h.shape[1]
    tiles = seq // ROW_TILE
    n_real_tiles = n_real // ROW_TILE
    blk = ROW_TILE // HALO
    meta_blk0 = n_real // HALO
    w_spec = _layer_resident(w_pool, layer)
    in_specs, operands = [], []
    if with_meta:
        meta0 = n_real // N_META
        seq_of = lambda b: jnp.minimum(b, nseq - 1)
        tail = pl.pallas_call(
            functools.partial(_pool_meta_kernel, nseq=nseq, seq=seq),
            grid=(ROW_TILE // N_META,),
            in_specs=[pl.BlockSpec((N_META, d), lambda b: (meta0 + seq_of(b), 0)),
                      pl.BlockSpec((HALO, d), lambda b: (seq_of(b) * tiles * blk, 0)),
                      _vec_spec(d), w_spec, _vec_spec(d)],
            out_specs=pl.BlockSpec((N_META, d), lambda b: (b, 0)),
            out_shape=jax.ShapeDtypeStruct((ROW_TILE, d), F32),
            scratch_shapes=[pltpu.VMEM((N_META + 2 * HALO, d), F32)],
            compiler_params=_params("arbitrary"),
            name="pool_mixer_meta",
        )(h, h, gain, w_pool, scale)
        in_specs, operands = [pl.BlockSpec((ROW_TILE, d), lambda i: (0, 0))], [tail]

    real = lambda i: jnp.minimum(i, n_real_tiles - 1)
    cur_spec = pl.BlockSpec((ROW_TILE, d), lambda i: (real(i), 0))
    prev_spec = pl.BlockSpec((HALO, d), lambda i: (
        jnp.where(real(i) % tiles == 0, meta_blk0 + 2 * (real(i) // tiles) + 1, real(i) * blk - 1), 0))
    next_spec = pl.BlockSpec((HALO, d), lambda i: (jnp.where(real(i) % tiles == tiles - 1, 0, (real(i) + 1) * blk), 0))
    n_out_tiles = n_real_tiles + (1 if with_meta else 0)
    return pl.pallas_call(
        functools.partial(_pool_kernel, seq=seq, n_real_tiles=n_real_tiles, with_tail=with_meta),
        grid=(n_out_tiles,),
        in_specs=[cur_spec, prev_spec, next_spec, _vec_spec(d), w_spec, _vec_spec(d)] + in_specs,
        out_specs=pl.BlockSpec((ROW_TILE, d), lambda i: (i, 0)),
        out_shape=jax.ShapeDtypeStruct((n_out_tiles * ROW_TILE, d), F32),
        scratch_shapes=[pltpu.VMEM((ROW_TILE + 2 * HALO, d), F32)],
        compiler_params=_params("parallel"),
        name="pool_mixer",
    )(h, h, h, gain, w_pool, scale, *operands)


def kernel(x_prompt, x_sample, meta_tokens, attn_norm, w_qkv, q_norm, k_norm, w_o, pool_norm, w_pool, pool_scale,
           mlp_norm, w_up, w_down, final_norm):
    b1, seq, d = x_prompt.shape
    b2 = x_sample.shape[0]
    assert x_sample.shape[1:] == (seq, d) and seq % ROW_TILE == 0 and seq % GRID_W == 0
    nseq = b1 + b2
    n_real = nseq * seq
    assert nseq * N_META <= ROW_TILE
    depth = mlp_norm.shape[0]
    t1, t2 = b1 * seq // ROW_TILE, b2 * seq // ROW_TILE

    tail = jnp.concatenate([jnp.tile(meta_tokens.astype(F32), (nseq, 1)),
                            jnp.zeros((ROW_TILE - nseq * N_META, d), F32)], axis=0)
    parts = [(x_prompt.reshape(b1 * seq, d), t1), (x_sample.reshape(b2 * seq, d), t2), (tail, 1)]

    nqk = (N_HEADS + N_KV_HEADS) * HEAD_DIM
    tables = _rope_tables(seq)
    w_qkv = jnp.concatenate([_split_pairs(w_qkv[..., :nqk], N_HEADS + N_KV_HEADS), w_qkv[..., nqk:]], axis=-1)
    w_qkv, w_o, w_pool, w_up, w_down = (w.astype(BF16) for w in (w_qkv, w_o, w_pool, w_up, w_down))
    row2 = lambda a: a.reshape(1, -1).astype(F32)
    q_gain = _split_pairs(q_norm.astype(F32), 1) * (HEAD_DIM ** -0.5 * math.log2(math.e))
    k_gain = _split_pairs(k_norm.astype(F32), 1)

    outs = None
    h = None
    for i in range(depth):
        j = i // 2
        last = i == depth - 1
        if i % 2 == 0:
            q, k, v = _qkv_call(parts, row2(attn_norm[j]), w_qkv, j, row2(q_gain[j]), row2(k_gain[j]), tables, seq, n_real)
            o_real, o_meta = _attn_call(q, k, v, nseq, seq, n_real)
            h = _oproj_call(parts, o_real, o_meta, w_o, j)
        else:
            h = _pool_call(h, row2(pool_norm[j]), w_pool, j, row2(pool_scale[j]), nseq, seq, n_real, with_meta=not last)
        if not last:
            h = _mlp_call(h, row2(mlp_norm[i]), w_up, w_down, i, row2(final_norm), 0, h.shape[0] // ROW_TILE, False)
            parts = [(h, h.shape[0] // ROW_TILE)]
        else:
            mlp = functools.partial(_mlp_call, h, row2(mlp_norm[i]), w_up, w_down, i, row2(final_norm))
            outs = (mlp(0, t1, True).reshape(b1, seq, d), mlp(t1, t2, True).reshape(b2, seq, d))
    return outs
```

```python
import functools
import math

import numpy as np
import jax
import jax.numpy as jnp
from jax import lax
from jax.experimental import pallas as pl
from jax.experimental.pallas import tpu as pltpu

N_HEADS = 16
N_KV_HEADS = 4
Q_PER_KV = N_HEADS // N_KV_HEADS
HEAD_DIM = 128
ROPE_THETA = 10000.0
GRID_W = 64
N_META = 16
POOL_WINDOWS = (2, 4, 8, 16)
EPS = 1e-6

LANES = 128
F32_SUBLANES = 8
HALO = max(POOL_WINDOWS) // 2
ROW_TILE = 512
FF_TILE = 1024
Q_TILE = 512
MXU_COLS = 256
VMEM_LIMIT_BYTES = 56 * 1024 * 1024

assert HALO == F32_SUBLANES and N_META == 2 * HALO and HEAD_DIM == LANES

BF16 = jnp.bfloat16
F32 = jnp.float32


def _params(*semantics):
    return pltpu.CompilerParams(dimension_semantics=semantics, vmem_limit_bytes=VMEM_LIMIT_BYTES)


def _rmsnorm(x, gain):
    ms = jnp.mean(x * x, axis=-1, keepdims=True)
    return x * lax.rsqrt(ms + EPS) * gain


def _layer_resident(w, layer):
    zeros = (0,) * (w.ndim - 1)
    return pl.BlockSpec((None,) + w.shape[1:], lambda *_: (layer,) + zeros, pipeline_mode=pl.Buffered(1))


def _vec_spec(width):
    return pl.BlockSpec((1, width), lambda *_: (0, 0))


def _row_part_specs(parts, d):
    specs, start = [], 0
    for _, n in parts:
        index_map = functools.partial(lambda i, *_, start, n: (jnp.clip(i - start, 0, n - 1), 0), start=start, n=n)
        mode = dict(pipeline_mode=pl.Buffered(1)) if n == 1 and len(parts) > 1 else {}
        specs.append(pl.BlockSpec((ROW_TILE, d), index_map, **mode))
        start += n
    return specs


def _read_row_tile(refs, counts):
    i = pl.program_id(0)
    x = refs[0][...]
    start = 0
    for ref, n in zip(refs[1:], counts[:-1]):
        start += n
        x = jnp.where(i >= start, ref[...], x)
    return x


def _split_pairs(w, n_heads):
    lead = w.shape[:-1]
    return w.reshape(*lead, n_heads, HEAD_DIM // 2, 2).swapaxes(-1, -2).reshape(*lead, n_heads * HEAD_DIM)


def _rope_tables(seq):
    t = np.arange((seq // GRID_W) * GRID_W)
    r = (t // GRID_W).astype(np.float32)
    c = (t % GRID_W).astype(np.float32)
    axis_dim = HEAD_DIM // 2
    inv_freq = (ROPE_THETA ** (-np.arange(0, axis_dim, 2, dtype=np.float32) / axis_dim)).astype(np.float32)
    ang = np.concatenate([r[:, None] * inv_freq[None], c[:, None] * inv_freq[None]], axis=-1)
    cos, sin = np.cos(ang).astype(np.float32), np.sin(ang).astype(np.float32)
    cos = np.concatenate([cos, cos], axis=-1)
    sin = np.concatenate([-sin, sin], axis=-1)
    ident = np.ones((ROW_TILE, HEAD_DIM), np.float32)
    zero = np.zeros((ROW_TILE, HEAD_DIM), np.float32)
    return jnp.asarray(np.concatenate([cos, ident])), jnp.asarray(np.concatenate([sin, zero]))


def _qkv_kernel(*refs, counts):
    n = len(counts)
    gain_ref, w_ref, qg_ref, kg_ref, cos_ref, sin_ref, q_ref, k_ref, v_ref, y_ref = refs[n:]
    i = pl.program_id(0)
    n_tiles = pl.num_programs(0) - 1
    nq, nk = q_ref.shape[1], k_ref.shape[1]

    def finish(col):
        gain = qg_ref[...] if col < nq else kg_ref[...]
        yn = _rmsnorm(y_ref[:, col:col + HEAD_DIM], gain)
        out = (yn * cos_ref[...] + pltpu.roll(yn, HEAD_DIM // 2, 1) * sin_ref[...]).astype(BF16)
        if col < nq:
            q_ref[:, col:col + HEAD_DIM] = out
        else:
            k_ref[:, col - nq:col - nq + HEAD_DIM] = out

    def step(project, finish_previous):
        if project:
            xn = _rmsnorm(_read_row_tile(refs[:n], counts), gain_ref[...]).astype(BF16)
        for c0 in range(0, nq + 2 * nk, MXU_COLS):
            head_cols = range(c0, c0 + MXU_COLS, HEAD_DIM)
            if finish_previous:
                for col in head_cols:
                    if col < nq + nk:
                        finish(col)
            if project:
                y = jnp.dot(xn, w_ref[:, c0:c0 + MXU_COLS], preferred_element_type=F32)
                for col in head_cols:
                    yh = y[:, col - c0:col - c0 + HEAD_DIM]
                    if col < nq + nk:
                        y_ref[:, col:col + HEAD_DIM] = yh
                    else:
                        v_ref[:, col - nq - nk:col - nq - nk + HEAD_DIM] = yh.astype(BF16)

    pl.when(i == 0)(functools.partial(step, True, False))
    pl.when((i > 0) & (i < n_tiles))(functools.partial(step, True, True))
    pl.when(i == n_tiles)(functools.partial(step, False, True))


def _qkv_call(parts, gain, w_qkv, layer, q_gain, k_gain, tables, seq, n_real):
    d = w_qkv.shape[1]
    counts = tuple(n for _, n in parts)
    n_tiles = sum(counts)
    ntok = n_tiles * ROW_TILE
    nq, nk = N_HEADS * HEAD_DIM, N_KV_HEADS * HEAD_DIM
    tiles_per_seq, n_real_tiles = seq // ROW_TILE, n_real // ROW_TILE
    finished = lambda i: jnp.maximum(i - 1, 0)
    row = lambda width, tile: pl.BlockSpec((ROW_TILE, width), lambda i: (tile(i), 0))
    table = pl.BlockSpec((ROW_TILE, HEAD_DIM), lambda i: (
        jnp.where(finished(i) < n_real_tiles, finished(i) % tiles_per_seq, tiles_per_seq), 0))
    return pl.pallas_call(
        functools.partial(_qkv_kernel, counts=counts),
        grid=(n_tiles + 1,),
        in_specs=_row_part_specs(parts, d) + [_vec_spec(d), _layer_resident(w_qkv, layer), _vec_spec(HEAD_DIM),
                                              _vec_spec(HEAD_DIM), table, table],
        out_specs=[row(nq, finished), row(nk, finished), row(nk, lambda i: jnp.minimum(i, n_tiles - 1))],
        out_shape=[jax.ShapeDtypeStruct((ntok, nq), BF16),
                   jax.ShapeDtypeStruct((ntok, nk), BF16),
                   jax.ShapeDtypeStruct((ntok, nk), BF16)],
        scratch_shapes=[pltpu.VMEM((ROW_TILE, nq + nk), F32)],
        compiler_params=_params("arbitrary"),
        name="qkv_proj",
    )(*[a for a, _ in parts], gain, w_qkv, q_gain, k_gain, *tables)


def _attn_kernel(q_ref, qm_ref, k_ref, km_ref, v_ref, vm_ref, zeros_ref, o_ref, om_ref, kpad_ref, vt_ref, vtpad_ref):
    del zeros_ref
    first_q_tile = pl.program_id(2) == 0

    @pl.when(first_q_tile)
    def _():
        pad = jnp.zeros((LANES - N_META, HEAD_DIM), BF16)
        kpad_ref[0:N_META, :] = km_ref[...]
        kpad_ref[N_META:, :] = pad
        vt_ref[...] = v_ref[...].astype(F32).T.astype(BF16)
        vtpad_ref[...] = jnp.concatenate([vm_ref[...], pad], axis=0).astype(F32).T.astype(BF16)

    def scores(qh):
        contract_last = (((1,), (1,)), ((), ()))
        st = lax.dot_general(k_ref[...], qh, contract_last, preferred_element_type=F32)
        stm = lax.dot_general(kpad_ref[...], qh, contract_last, preferred_element_type=F32)
        key = lax.broadcasted_iota(jnp.int32, stm.shape, 0)
        return st, jnp.where(key < N_META, stm, -jnp.inf)

    def softmax_pv(st, stm):
        m = jnp.maximum(jnp.max(st, axis=0, keepdims=True), jnp.max(stm, axis=0, keepdims=True))
        p, pm = jnp.exp2(st - m), jnp.exp2(stm - m)
        denom = jnp.sum(p, axis=0, keepdims=True) + jnp.sum(pm, axis=0, keepdims=True)
        ot = (jnp.dot(vt_ref[...], p.astype(BF16), preferred_element_type=F32)
              + jnp.dot(vtpad_ref[...], pm.astype(BF16), preferred_element_type=F32))
        return (ot * (1.0 / denom)).T

    def attend(qh):
        return softmax_pv(*scores(qh))

    head_cols = [slice(hq * HEAD_DIM, (hq + 1) * HEAD_DIM) for hq in range(Q_PER_KV)]
    head_scores = [scores(q_ref[:, cols]) for cols in head_cols]
    for cols, s in zip(head_cols, head_scores):
        o_ref[:, cols] = softmax_pv(*s).astype(BF16)

    @pl.when(first_q_tile)
    def _():
        qm = [qm_ref[:, hq * HEAD_DIM:(hq + 1) * HEAD_DIM] for hq in range(Q_PER_KV)]
        qm.append(jnp.zeros((LANES - Q_PER_KV * N_META, HEAD_DIM), BF16))
        om = attend(jnp.concatenate(qm, axis=0))
        for hq in range(Q_PER_KV):
            om_ref[:, hq * HEAD_DIM:(hq + 1) * HEAD_DIM] = om[hq * N_META:(hq + 1) * N_META].astype(BF16)


def _attn_call(q, k, v, nseq, seq, n_real):
    gw = Q_PER_KV * HEAD_DIM
    q_tiles = seq // Q_TILE
    meta0 = n_real // N_META
    q_spec = pl.BlockSpec((Q_TILE, gw), lambda b, g, i: (b * q_tiles + i, g))
    qm_spec = pl.BlockSpec((N_META, gw), lambda b, g, i: (meta0 + b, g))
    kv_spec = pl.BlockSpec((seq, HEAD_DIM), lambda b, g, i: (b, g))
    kvm_spec = pl.BlockSpec((N_META, HEAD_DIM), lambda b, g, i: (meta0 + b, g))
    om_spec = pl.BlockSpec((N_META, gw), lambda b, g, i: (b, g))
    zeros = jnp.zeros((ROW_TILE, N_HEADS * HEAD_DIM), BF16)
    return pl.pallas_call(
        _attn_kernel,
        grid=(nseq, N_KV_HEADS, q_tiles),
        in_specs=[q_spec, qm_spec, kv_spec, kvm_spec, kv_spec, kvm_spec, pl.BlockSpec(memory_space=pl.ANY)],
        out_specs=[q_spec, om_spec],
        out_shape=[jax.ShapeDtypeStruct((n_real, N_HEADS * HEAD_DIM), BF16),
                   jax.ShapeDtypeStruct(zeros.shape, BF16)],
        scratch_shapes=[pltpu.VMEM((LANES, HEAD_DIM), BF16), pltpu.VMEM((HEAD_DIM, seq), BF16),
                        pltpu.VMEM((HEAD_DIM, LANES), BF16)],
        input_output_aliases={6: 1},
        compiler_params=_params("parallel", "parallel", "arbitrary"),
        name="attention",
    )(q, q, k, k, v, v, zeros)


def _oproj_kernel(*refs, counts):
    n = len(counts)
    o_ref, om_ref, w_ref, out_ref = refs[n:]
    is_meta_tile = pl.program_id(0) == pl.num_programs(0) - 1
    o = jnp.where(is_meta_tile, om_ref[...], o_ref[...])
    out_ref[...] = _read_row_tile(refs[:n], counts) + jnp.dot(o, w_ref[...], preferred_element_type=F32)


def _oproj_call(parts, o_real, o_meta, w_o, layer):
    d = w_o.shape[2]
    counts = tuple(n for _, n in parts)
    n_tiles = sum(counts)
    n_real_tiles = o_real.shape[0] // ROW_TILE
    o_spec = pl.BlockSpec((ROW_TILE, o_real.shape[1]), lambda i: (jnp.minimum(i, n_real_tiles - 1), 0))
    om_spec = pl.BlockSpec((ROW_TILE, o_real.shape[1]), lambda i: (0, 0))
    return pl.pallas_call(
        functools.partial(_oproj_kernel, counts=counts),
        grid=(n_tiles,),
        in_specs=_row_part_specs(parts, d) + [o_spec, om_spec, _layer_resident(w_o, layer)],
        out_specs=pl.BlockSpec((ROW_TILE, d), lambda i: (i, 0)),
        out_shape=jax.ShapeDtypeStruct((n_tiles * ROW_TILE, d), F32),
        compiler_params=_params("parallel"),
        name="attn_out_proj",
    )(*[a for a, _ in parts], o_real, o_meta, w_o)


def _mlp_kernel(h_ref, gain_ref, wu_ref, wd_ref, fgain_ref, out_ref, xn_ref, act_ref, *, k_steps, final_norm):
    s = pl.program_id(0)
    k = s % k_steps
    first, last = s == 0, s == pl.num_programs(0) - 1

    def normalise():
        xn_ref[...] = _rmsnorm(h_ref[...], gain_ref[...]).astype(BF16)

    def up():
        u = jnp.dot(xn_ref[...], wu_ref[...], preferred_element_type=F32)
        act_ref[s % 2] = jnp.square(jnp.maximum(u, 0.0)).astype(BF16)

    def down():
        return jnp.dot(act_ref[(s + 1) % 2], wd_ref[...], preferred_element_type=F32)

    def finish_tile():
        y = out_ref[...] + down()
        out_ref[...] = _rmsnorm(y, fgain_ref[...]) if final_norm else y

    @pl.when(first)
    def _():
        normalise()
        up()

    @pl.when((k == 0) & ~first & ~last)
    def _():
        finish_tile()
        normalise()
        up()

    @pl.when(k == 1)
    def _():
        out_ref[...] = h_ref[...] + down()
        up()

    @pl.when(k > 1)
    def _():
        out_ref[...] += down()
        up()

    @pl.when(last)
    def _():
        finish_tile()


def _mlp_call(h, gain, w_up, w_down, layer, final_gain, first_tile, n_tiles, final_norm):
    _, d, dff = w_up.shape
    k_steps = dff // FF_TILE
    assert k_steps >= 2
    n_steps = n_tiles * k_steps + 1
    up_tile = lambda s: (s // k_steps, s % k_steps)
    down_tile = lambda s: up_tile(jnp.maximum(s - 1, 0))
    return pl.pallas_call(
        functools.partial(_mlp_kernel, k_steps=k_steps, final_norm=final_norm),
        grid=(n_steps,),
        in_specs=[pl.BlockSpec((ROW_TILE, d), lambda s: (first_tile + jnp.minimum(up_tile(s)[0], n_tiles - 1), 0)),
                  _vec_spec(d),
                  pl.BlockSpec((None, d, FF_TILE), lambda s: (layer, 0, up_tile(s)[1])),
                  pl.BlockSpec((None, FF_TILE, d), lambda s: (layer, down_tile(s)[1], 0)), _vec_spec(d)],
        out_specs=pl.BlockSpec((ROW_TILE, d), lambda s: (down_tile(s)[0], 0)),
        out_shape=jax.ShapeDtypeStruct((n_tiles * ROW_TILE, d), F32),
        scratch_shapes=[pltpu.VMEM((ROW_TILE, d), BF16), pltpu.VMEM((2, ROW_TILE, FF_TILE), BF16)],
        compiler_params=_params("arbitrary"),
        name="mlp_final" if final_norm else "mlp",
    )(h, gain, w_up, w_down, final_gain)


def _pool_rows(x, xe_ref, first_pos, seq_len, w_ref, scale_ref):
    n = x.shape[0]
    gdim = w_ref.shape[1]
    pos = first_pos + lax.broadcasted_iota(jnp.int32, (n, LANES), 0)
    outs = []
    for g, w in enumerate(POOL_WINDOWS):
        cols = slice(g * gdim, (g + 1) * gdim)
        acc = xe_ref[HALO - w // 2:HALO - w // 2 + n, cols]
        for dlt in range(1 - w // 2, w // 2):
            acc = acc + xe_ref[HALO + dlt:HALO + dlt + n, cols]
        cnt = (jnp.minimum(pos + w // 2, seq_len) - jnp.maximum(pos - w // 2, 0)).astype(F32)
        inv = jnp.tile(1.0 / cnt, (1, gdim // LANES))
        mixed = (acc * inv - xe_ref[HALO:HALO + n, cols]).astype(BF16)
        y = jnp.dot(mixed, w_ref[g], preferred_element_type=F32)
        outs.append(x[:, cols] + y * scale_ref[:, cols])
    return jnp.concatenate(outs, axis=-1)


def _pool_kernel(*refs, seq, n_real_tiles, with_tail):
    cur_ref, prev_ref, next_ref, gain_ref, w_ref, scale_ref = refs[:6]
    out_ref, xe_ref = refs[-2:]
    i = pl.program_id(0)
    tiles = seq // ROW_TILE
    j = i % tiles

    @pl.when(i < n_real_tiles)
    def _():
        gain = gain_ref[...]
        x = cur_ref[...]
        n = x.shape[0]
        xe_ref[0:HALO, :] = _rmsnorm(prev_ref[...], gain)
        xe_ref[HALO:HALO + n, :] = _rmsnorm(x, gain)
        nxt = _rmsnorm(next_ref[...], gain)
        xe_ref[HALO + n:, :] = jnp.where(j == tiles - 1, 0.0, nxt)
        out_ref[...] = _pool_rows(x, xe_ref, N_META + j * n, N_META + seq, w_ref, scale_ref)

    if with_tail:
        @pl.when(i == n_real_tiles)
        def _():
            out_ref[...] = refs[6][...]


def _pool_meta_kernel(cur_ref, next_ref, gain_ref, w_ref, scale_ref, out_ref, xe_ref, *, nseq, seq):
    b = pl.program_id(0)

    @pl.when(b < nseq)
    def _():
        gain = gain_ref[...]
        x = cur_ref[...]
        xe_ref[0:HALO, :] = jnp.zeros((HALO, x.shape[1]), F32)
        xe_ref[HALO:HALO + N_META, :] = _rmsnorm(x, gain)
        xe_ref[HALO + N_META:, :] = _rmsnorm(next_ref[...], gain)
        out_ref[...] = _pool_rows(x, xe_ref, 0, N_META + seq, w_ref, scale_ref)

    @pl.when(b >= nseq)
    def _():
        out_ref[...] = jnp.zeros(out_ref.shape, F32)


def _pool_call(h, gain, w_pool, layer, scale, nseq, seq, n_real, with_meta):
    d = h.shape[1]
    tiles = seq // ROW_TILE
    n_real_tiles = n_real // ROW_TILE
    blk = ROW_TILE // HALO
    meta_blk0 = n_real // HALO
    w_spec = _layer_resident(w_pool, layer)
    in_specs, operands = [], []
    if with_meta:
        meta0 = n_real // N_META
        seq_of = lambda b: jnp.minimum(b, nseq - 1)
        tail = pl.pallas_call(
            functools.partial(_pool_meta_kernel, nseq=nseq, seq=seq),
            grid=(ROW_TILE // N_META,),
            in_specs=[pl.BlockSpec((N_META, d), lambda b: (meta0 + seq_of(b), 0)),
                      pl.BlockSpec((HALO, d), lambda b: (seq_of(b) * tiles * blk, 0)),
                      _vec_spec(d), w_spec, _vec_spec(d)],
            out_specs=pl.BlockSpec((N_META, d), lambda b: (b, 0)),
            out_shape=jax.ShapeDtypeStruct((ROW_TILE, d), F32),
            scratch_shapes=[pltpu.VMEM((N_META + 2 * HALO, d), F32)],
            compiler_params=_params("arbitrary"),
            name="pool_mixer_meta",
        )(h, h, gain, w_pool, scale)
        in_specs, operands = [pl.BlockSpec((ROW_TILE, d), lambda i: (0, 0))], [tail]

    real = lambda i: jnp.minimum(i, n_real_tiles - 1)
    cur_spec = pl.BlockSpec((ROW_TILE, d), lambda i: (real(i), 0))
    prev_spec = pl.BlockSpec((HALO, d), lambda i: (
        jnp.where(real(i) % tiles == 0, meta_blk0 + 2 * (real(i) // tiles) + 1, real(i) * blk - 1), 0))
    next_spec = pl.BlockSpec((HALO, d), lambda i: (jnp.where(real(i) % tiles == tiles - 1, 0, (real(i) + 1) * blk), 0))
    n_out_tiles = n_real_tiles + (1 if with_meta else 0)
    return pl.pallas_call(
        functools.partial(_pool_kernel, seq=seq, n_real_tiles=n_real_tiles, with_tail=with_meta),
        grid=(n_out_tiles,),
        in_specs=[cur_spec, prev_spec, next_spec, _vec_spec(d), w_spec, _vec_spec(d)] + in_specs,
        out_specs=pl.BlockSpec((ROW_TILE, d), lambda i: (i, 0)),
        out_shape=jax.ShapeDtypeStruct((n_out_tiles * ROW_TILE, d), F32),
        scratch_shapes=[pltpu.VMEM((ROW_TILE + 2 * HALO, d), F32)],
        compiler_params=_params("parallel"),
        name="pool_mixer",
    )(h, h, h, gain, w_pool, scale, *operands)


def kernel(x_prompt, x_sample, meta_tokens, attn_norm, w_qkv, q_norm, k_norm, w_o, pool_norm, w_pool, pool_scale,
           mlp_norm, w_up, w_down, final_norm):
    b1, seq, d = x_prompt.shape
    b2 = x_sample.shape[0]
    assert x_sample.shape[1:] == (seq, d) and seq % ROW_TILE == 0 and seq % GRID_W == 0
    nseq = b1 + b2
    n_real = nseq * seq
    assert nseq * N_META <= ROW_TILE
    depth = mlp_norm.shape[0]
    t1, t2 = b1 * seq // ROW_TILE, b2 * seq // ROW_TILE

    tail = jnp.concatenate([jnp.tile(meta_tokens.astype(F32), (nseq, 1)),
                            jnp.zeros((ROW_TILE - nseq * N_META, d), F32)], axis=0)
    parts = [(x_prompt.reshape(b1 * seq, d), t1), (x_sample.reshape(b2 * seq, d), t2), (tail, 1)]

    nqk = (N_HEADS + N_KV_HEADS) * HEAD_DIM
    tables = _rope_tables(seq)
    w_qkv = jnp.concatenate([_split_pairs(w_qkv[..., :nqk], N_HEADS + N_KV_HEADS), w_qkv[..., nqk:]], axis=-1)
    w_qkv, w_o, w_pool, w_up, w_down = (w.astype(BF16) for w in (w_qkv, w_o, w_pool, w_up, w_down))
    row2 = lambda a: a.reshape(1, -1).astype(F32)
    q_gain = _split_pairs(q_norm.astype(F32), 1) * (HEAD_DIM ** -0.5 * math.log2(math.e))
    k_gain = _split_pairs(k_norm.astype(F32), 1)

    outs = None
    h = None
    for i in range(depth):
        j = i // 2
        last = i == depth - 1
        if i % 2 == 0:
            q, k, v = _qkv_call(parts, row2(attn_norm[j]), w_qkv, j, row2(q_gain[j]), row2(k_gain[j]), tables, seq, n_real)
            o_real, o_meta = _attn_call(q, k, v, nseq, seq, n_real)
            h = _oproj_call(parts, o_real, o_meta, w_o, j)
        else:
            h = _pool_call(h, row2(pool_norm[j]), w_pool, j, row2(pool_scale[j]), nseq, seq, n_real, with_meta=not last)
        if not last:
            h = _mlp_call(h, row2(mlp_norm[i]), w_up, w_down, i, row2(final_norm), 0, h.shape[0] // ROW_TILE, False)
            parts = [(h, h.shape[0] // ROW_TILE)]
        else:
            mlp = functools.partial(_mlp_call, h, row2(mlp_norm[i]), w_up, w_down, i, row2(final_norm))
            outs = (mlp(0, t1, True).reshape(b1, seq, d), mlp(t1, t2, True).reshape(b2, seq, d))
    return outs
```

```python
import functools
import math

import numpy as np
import jax
import jax.numpy as jnp
from jax import lax
from jax.experimental import pallas as pl
from jax.experimental.pallas import tpu as pltpu

N_HEADS = 16
N_KV_HEADS = 4
Q_PER_KV = N_HEADS // N_KV_HEADS
HEAD_DIM = 128
ROPE_THETA = 10000.0
GRID_W = 64
N_META = 16
POOL_WINDOWS = (2, 4, 8, 16)
EPS = 1e-6

LANES = 128
F32_SUBLANES = 8
HALO = max(POOL_WINDOWS) // 2
ROW_TILE = 512
FF_TILE = 1024
Q_TILE = 512
MXU_COLS = 256
VMEM_LIMIT_BYTES = 56 * 1024 * 1024

assert HALO == F32_SUBLANES and N_META == 2 * HALO and HEAD_DIM == LANES

BF16 = jnp.bfloat16
F32 = jnp.float32


def _params(*semantics):
    return pltpu.CompilerParams(dimension_semantics=semantics, vmem_limit_bytes=VMEM_LIMIT_BYTES)


def _rmsnorm(x, gain):
    ms = jnp.mean(x * x, axis=-1, keepdims=True)
    return x * lax.rsqrt(ms + EPS) * gain


def _layer_resident(w, layer):
    zeros = (0,) * (w.ndim - 1)
    return pl.BlockSpec((None,) + w.shape[1:], lambda *_: (layer,) + zeros, pipeline_mode=pl.Buffered(1))


def _vec_spec(width):
    return pl.BlockSpec((1, width), lambda *_: (0, 0))


def _row_part_specs(parts, d):
    specs, start = [], 0
    for _, n in parts:
        index_map = functools.partial(lambda i, *_, start, n: (jnp.clip(i - start, 0, n - 1), 0), start=start, n=n)
        mode = dict(pipeline_mode=pl.Buffered(1)) if n == 1 and len(parts) > 1 else {}
        specs.append(pl.BlockSpec((ROW_TILE, d), index_map, **mode))
        start += n
    return specs


def _read_row_tile(refs, counts):
    i = pl.program_id(0)
    x = refs[0][...]
    start = 0
    for ref, n in zip(refs[1:], counts[:-1]):
        start += n
        x = jnp.where(i >= start, ref[...], x)
    return x


def _split_pairs(w, n_heads):
    lead = w.shape[:-1]
    return w.reshape(*lead, n_heads, HEAD_DIM // 2, 2).swapaxes(-1, -2).reshape(*lead, n_heads * HEAD_DIM)


def _rope_tables(seq):
    t = np.arange((seq // GRID_W) * GRID_W)
    r = (t // GRID_W).astype(np.float32)
    c = (t % GRID_W).astype(np.float32)
    axis_dim = HEAD_DIM // 2
    inv_freq = (ROPE_THETA ** (-np.arange(0, axis_dim, 2, dtype=np.float32) / axis_dim)).astype(np.float32)
    ang = np.concatenate([r[:, None] * inv_freq[None], c[:, None] * inv_freq[None]], axis=-1)
    cos, sin = np.cos(ang).astype(np.float32), np.sin(ang).astype(np.float32)
    cos = np.concatenate([cos, cos], axis=-1)
    sin = np.concatenate([-sin, sin], axis=-1)
    ident = np.ones((ROW_TILE, HEAD_DIM), np.float32)
    zero = np.zeros((ROW_TILE, HEAD_DIM), np.float32)
    return jnp.asarray(np.concatenate([cos, ident])), jnp.asarray(np.concatenate([sin, zero]))


def _qkv_kernel(*refs, counts):
    n = len(counts)
    gain_ref, w_ref, qg_ref, kg_ref, cos_ref, sin_ref, q_ref, k_ref, v_ref, y_ref = refs[n:]
    i = pl.program_id(0)
    n_tiles = pl.num_programs(0) - 1
    nq, nk = q_ref.shape[1], k_ref.shape[1]

    def finish(col):
        gain = qg_ref[...] if col < nq else kg_ref[...]
        yn = _rmsnorm(y_ref[:, col:col + HEAD_DIM], gain)
        out = (yn * cos_ref[...] + pltpu.roll(yn, HEAD_DIM // 2, 1) * sin_ref[...]).astype(BF16)
        if col < nq:
            q_ref[:, col:col + HEAD_DIM] = out
        else:
            k_ref[:, col - nq:col - nq + HEAD_DIM] = out

    def step(project, finish_previous):
        if project:
            xn = _rmsnorm(_read_row_tile(refs[:n], counts), gain_ref[...]).astype(BF16)
        for c0 in range(0, nq + 2 * nk, MXU_COLS):
            head_cols = range(c0, c0 + MXU_COLS, HEAD_DIM)
            if finish_previous:
                for col in head_cols:
                    if col < nq + nk:
                        finish(col)
            if project:
                y = jnp.dot(xn, w_ref[:, c0:c0 + MXU_COLS], preferred_element_type=F32)
                for col in head_cols:
                    yh = y[:, col - c0:col - c0 + HEAD_DIM]
                    if col < nq + nk:
                        y_ref[:, col:col + HEAD_DIM] = yh
                    else:
                        v_ref[:, col - nq - nk:col - nq - nk + HEAD_DIM] = yh.astype(BF16)

    pl.when(i == 0)(functools.partial(step, True, False))
    pl.when((i > 0) & (i < n_tiles))(functools.partial(step, True, True))
    pl.when(i == n_tiles)(functools.partial(step, False, True))


def _qkv_call(parts, gain, w_qkv, layer, q_gain, k_gain, tables, seq, n_real):
    d = w_qkv.shape[1]
    counts = tuple(n for _, n in parts)
    n_tiles = sum(counts)
    ntok = n_tiles * ROW_TILE
    nq, nk = N_HEADS * HEAD_DIM, N_KV_HEADS * HEAD_DIM
    tiles_per_seq, n_real_tiles = seq // ROW_TILE, n_real // ROW_TILE
    finished = lambda i: jnp.maximum(i - 1, 0)
    row = lambda width, tile: pl.BlockSpec((ROW_TILE, width), lambda i: (tile(i), 0))
    table = pl.BlockSpec((ROW_TILE, HEAD_DIM), lambda i: (
        jnp.where(finished(i) < n_real_tiles, finished(i) % tiles_per_seq, tiles_per_seq), 0))
    return pl.pallas_call(
        functools.partial(_qkv_kernel, counts=counts),
        grid=(n_tiles + 1,),
        in_specs=_row_part_specs(parts, d) + [_vec_spec(d), _layer_resident(w_qkv, layer), _vec_spec(HEAD_DIM),
                                              _vec_spec(HEAD_DIM), table, table],
        out_specs=[row(nq, finished), row(nk, finished), row(nk, lambda i: jnp.minimum(i, n_tiles - 1))],
        out_shape=[jax.ShapeDtypeStruct((ntok, nq), BF16),
                   jax.ShapeDtypeStruct((ntok, nk), BF16),
                   jax.ShapeDtypeStruct((ntok, nk), BF16)],
        scratch_shapes=[pltpu.VMEM((ROW_TILE, nq + nk), F32)],
        compiler_params=_params("arbitrary"),
        name="qkv_proj",
    )(*[a for a, _ in parts], gain, w_qkv, q_gain, k_gain, *tables)


def _attn_kernel(q_ref, qm_ref, k_ref, km_ref, v_ref, vm_ref, zeros_ref, o_ref, om_ref, kpad_ref, vt_ref, vtpad_ref):
    del zeros_ref
    first_q_tile = pl.program_id(2) == 0

    @pl.when(first_q_tile)
    def _():
        pad = jnp.zeros((LANES - N_META, HEAD_DIM), BF16)
        kpad_ref[0:N_META, :] = km_ref[...]
        kpad_ref[N_META:, :] = pad
        vt_ref[...] = v_ref[...].astype(F32).T.astype(BF16)
        vtpad_ref[...] = jnp.concatenate([vm_ref[...], pad], axis=0).astype(F32).T.astype(BF16)

    def scores(qh):
        contract_last = (((1,), (1,)), ((), ()))
        st = lax.dot_general(k_ref[...], qh, contract_last, preferred_element_type=F32)
        stm = lax.dot_general(kpad_ref[...], qh, contract_last, preferred_element_type=F32)
        key = lax.broadcasted_iota(jnp.int32, stm.shape, 0)
        return st, jnp.where(key < N_META, stm, -jnp.inf)

    def softmax_pv(st, stm):
        m = jnp.maximum(jnp.max(st, axis=0, keepdims=True), jnp.max(stm, axis=0, keepdims=True))
        p, pm = jnp.exp2(st - m), jnp.exp2(stm - m)
        denom = jnp.sum(p, axis=0, keepdims=True) + jnp.sum(pm, axis=0, keepdims=True)
        ot = (jnp.dot(vt_ref[...], p.astype(BF16), preferred_element_type=F32)
              + jnp.dot(vtpad_ref[...], pm.astype(BF16), preferred_element_type=F32))
        return (ot * (1.0 / denom)).T

    def attend(qh):
        return softmax_pv(*scores(qh))

    head_cols = [slice(hq * HEAD_DIM, (hq + 1) * HEAD_DIM) for hq in range(Q_PER_KV)]
    head_scores = [scores(q_ref[:, cols]) for cols in head_cols]
    for cols, s in zip(head_cols, head_scores):
        o_ref[:, cols] = softmax_pv(*s).astype(BF16)

    @pl.when(first_q_tile)
    def _():
        qm = [qm_ref[:, hq * HEAD_DIM:(hq + 1) * HEAD_DIM] for hq in range(Q_PER_KV)]
        qm.append(jnp.zeros((LANES - Q_PER_KV * N_META, HEAD_DIM), BF16))
        om = attend(jnp.concatenate(qm, axis=0))
        for hq in range(Q_PER_KV):
            om_ref[:, hq * HEAD_DIM:(hq + 1) * HEAD_DIM] = om[hq * N_META:(hq + 1) * N_META].astype(BF16)


def _attn_call(q, k, v, nseq, seq, n_real):
    gw = Q_PER_KV * HEAD_DIM
    q_tiles = seq // Q_TILE
    meta0 = n_real // N_META
    q_spec = pl.BlockSpec((Q_TILE, gw), lambda b, g, i: (b * q_tiles + i, g))
    qm_spec = pl.BlockSpec((N_META, gw), lambda b, g, i: (meta0 + b, g))
    kv_spec = pl.BlockSpec((seq, HEAD_DIM), lambda b, g, i: (b, g))
    kvm_spec = pl.BlockSpec((N_META, HEAD_DIM), lambda b, g, i: (meta0 + b, g))
    om_spec = pl.BlockSpec((N_META, gw), lambda b, g, i: (b, g))
    zeros = jnp.zeros((ROW_TILE, N_HEADS * HEAD_DIM), BF16)
    return pl.pallas_call(
        _attn_kernel,
        grid=(nseq, N_KV_HEADS, q_tiles),
        in_specs=[q_spec, qm_spec, kv_spec, kvm_spec, kv_spec, kvm_spec, pl.BlockSpec(memory_space=pl.ANY)],
        out_specs=[q_spec, om_spec],
        out_shape=[jax.ShapeDtypeStruct((n_real, N_HEADS * HEAD_DIM), BF16),
                   jax.ShapeDtypeStruct(zeros.shape, BF16)],
        scratch_shapes=[pltpu.VMEM((LANES, HEAD_DIM), BF16), pltpu.VMEM((HEAD_DIM, seq), BF16),
                        pltpu.VMEM((HEAD_DIM, LANES), BF16)],
        input_output_aliases={6: 1},
        compiler_params=_params("parallel", "parallel", "arbitrary"),
        name="attention",
    )(q, q, k, k, v, v, zeros)


def _oproj_kernel(*refs, counts):
    n = len(counts)
    o_ref, om_ref, w_ref, out_ref = refs[n:]
    is_meta_tile = pl.program_id(0) == pl.num_programs(0) - 1
    o = jnp.where(is_meta_tile, om_ref[...], o_ref[...])
    out_ref[...] = _read_row_tile(refs[:n], counts) + jnp.dot(o, w_ref[...], preferred_element_type=F32)


def _oproj_call(parts, o_real, o_meta, w_o, layer):
    d = w_o.shape[2]
    counts = tuple(n for _, n in parts)
    n_tiles = sum(counts)
    n_real_tiles = o_real.shape[0] // ROW_TILE
    o_spec = pl.BlockSpec((ROW_TILE, o_real.shape[1]), lambda i: (jnp.minimum(i, n_real_tiles - 1), 0))
    om_spec = pl.BlockSpec((ROW_TILE, o_real.shape[1]), lambda i: (0, 0))
    return pl.pallas_call(
        functools.partial(_oproj_kernel, counts=counts),
        grid=(n_tiles,),
        in_specs=_row_part_specs(parts, d) + [o_spec, om_spec, _layer_resident(w_o, layer)],
        out_specs=pl.BlockSpec((ROW_TILE, d), lambda i: (i, 0)),
        out_shape=jax.ShapeDtypeStruct((n_tiles * ROW_TILE, d), F32),
        compiler_params=_params("parallel"),
        name="attn_out_proj",
    )(*[a for a, _ in parts], o_real, o_meta, w_o)


def _mlp_kernel(h_ref, gain_ref, wu_hbm, wd_hbm, fgain_ref, out_ref, xn_ref, wu_buf, wd_buf, sems, *, layer,
                final_norm):
    i = pl.program_id(0)
    ff = wu_buf.shape[2]
    k_steps = wu_hbm.shape[2] // ff

    def chunk_copies(k, slot):
        col = pl.multiple_of(k * ff, ff)
        return (pltpu.make_async_copy(wu_hbm.at[layer, :, pl.ds(col, ff)], wu_buf.at[slot], sems.at[0, slot]),
                pltpu.make_async_copy(wd_hbm.at[layer, pl.ds(col, ff), :], wd_buf.at[slot], sems.at[1, slot]))

    def start(k, slot):
        for copy in chunk_copies(k, slot):
            copy.start()

    pl.when(i == 0)(functools.partial(start, 0, 0))

    x = h_ref[...]
    xn_ref[...] = _rmsnorm(x, gain_ref[...]).astype(BF16)
    out_ref[...] = x

    def pair(j, carry):
        for slot in range(2):
            k = 2 * j + slot
            for copy in chunk_copies(k, slot):
                copy.wait()
            if slot == 0:
                start(k + 1, 1)
            else:
                more = (j + 1 < k_steps // 2) | (i + 1 < pl.num_programs(0))
                pl.when(more)(functools.partial(start, (k + 1) % k_steps, 0))
            u = jnp.dot(xn_ref[...], wu_buf[slot], preferred_element_type=F32)
            a = jnp.square(jnp.maximum(u, 0.0)).astype(BF16)
            out_ref[...] += jnp.dot(a, wd_buf[slot], preferred_element_type=F32)
        return carry

    lax.fori_loop(0, k_steps // 2, pair, 0)
    if final_norm:
        out_ref[...] = _rmsnorm(out_ref[...], fgain_ref[...])


def _mlp_call(h, gain, w_up, w_down, layer, final_gain, first_tile, n_tiles, final_norm):
    _, d, dff = w_up.shape
    assert (dff // FF_TILE) % 2 == 0
    return pl.pallas_call(
        functools.partial(_mlp_kernel, layer=layer, final_norm=final_norm),
        grid=(n_tiles,),
        in_specs=[pl.BlockSpec((ROW_TILE, d), lambda i: (first_tile + i, 0)), _vec_spec(d),
                  pl.BlockSpec(memory_space=pl.ANY), pl.BlockSpec(memory_space=pl.ANY), _vec_spec(d)],
        out_specs=pl.BlockSpec((ROW_TILE, d), lambda i: (i, 0)),
        out_shape=jax.ShapeDtypeStruct((n_tiles * ROW_TILE, d), F32),
        scratch_shapes=[pltpu.VMEM((ROW_TILE, d), BF16), pltpu.VMEM((2, d, FF_TILE), BF16),
                        pltpu.VMEM((2, FF_TILE, d), BF16), pltpu.SemaphoreType.DMA((2, 2))],
        compiler_params=_params("arbitrary"),
        name="mlp_final" if final_norm else "mlp",
    )(h, gain, w_up, w_down, final_gain)


def _pool_rows(x, xe_ref, first_pos, seq_len, w_ref, scale_ref):
    n = x.shape[0]
    gdim = w_ref.shape[1]
    pos = first_pos + lax.broadcasted_iota(jnp.int32, (n, LANES), 0)
    outs = []
    for g, w in enumerate(POOL_WINDOWS):
        cols = slice(g * gdim, (g + 1) * gdim)
        acc = xe_ref[HALO - w // 2:HALO - w // 2 + n, cols]
        for dlt in range(1 - w // 2, w // 2):
            acc = acc + xe_ref[HALO + dlt:HALO + dlt + n, cols]
        cnt = (jnp.minimum(pos + w // 2, seq_len) - jnp.maximum(pos - w // 2, 0)).astype(F32)
        inv = jnp.tile(1.0 / cnt, (1, gdim // LANES))
        mixed = (acc * inv - xe_ref[HALO:HALO + n, cols]).astype(BF16)
        y = jnp.dot(mixed, w_ref[g], preferred_element_type=F32)
        outs.append(x[:, cols] + y * scale_ref[:, cols])
    return jnp.concatenate(outs, axis=-1)


def _pool_kernel(*refs, seq, n_real_tiles, with_tail):
    cur_ref, prev_ref, next_ref, gain_ref, w_ref, scale_ref = refs[:6]
    out_ref, xe_ref = refs[-2:]
    i = pl.program_id(0)
    tiles = seq // ROW_TILE
    j = i % tiles

    @pl.when(i < n_real_tiles)
    def _():
        gain = gain_ref[...]
        x = cur_ref[...]
        n = x.shape[0]
        xe_ref[0:HALO, :] = _rmsnorm(prev_ref[...], gain)
        xe_ref[HALO:HALO + n, :] = _rmsnorm(x, gain)
        nxt = _rmsnorm(next_ref[...], gain)
        xe_ref[HALO + n:, :] = jnp.where(j == tiles - 1, 0.0, nxt)
        out_ref[...] = _pool_rows(x, xe_ref, N_META + j * n, N_META + seq, w_ref, scale_ref)

    if with_tail:
        @pl.when(i == n_real_tiles)
        def _():
            out_ref[...] = refs[6][...]


def _pool_meta_kernel(cur_ref, next_ref, gain_ref, w_ref, scale_ref, out_ref, xe_ref, *, nseq, seq):
    b = pl.program_id(0)

    @pl.when(b < nseq)
    def _():
        gain = gain_ref[...]
        x = cur_ref[...]
        xe_ref[0:HALO, :] = jnp.zeros((HALO, x.shape[1]), F32)
        xe_ref[HALO:HALO + N_META, :] = _rmsnorm(x, gain)
        xe_ref[HALO + N_META:, :] = _rmsnorm(next_ref[...], gain)
        out_ref[...] = _pool_rows(x, xe_ref, 0, N_META + seq, w_ref, scale_ref)

    @pl.when(b >= nseq)
    def _():
        out_ref[...] = jnp.zeros(out_ref.shape, F32)


def _pool_call(h, gain, w_pool, layer, scale, nseq, seq, n_real, with_meta):
    d = h.shape[1]
    tiles = seq // ROW_TILE
    n_real_tiles = n_real // ROW_TILE
    blk = ROW_TILE // HALO
    meta_blk0 = n_real // HALO
    w_spec = _layer_resident(w_pool, layer)
    in_specs, operands = [], []
    if with_meta:
        meta0 = n_real // N_META
        seq_of = lambda b: jnp.minimum(b, nseq - 1)
        tail = pl.pallas_call(
            functools.partial(_pool_meta_kernel, nseq=nseq, seq=seq),
            grid=(ROW_TILE // N_META,),
            in_specs=[pl.BlockSpec((N_META, d), lambda b: (meta0 + seq_of(b), 0)),
                      pl.BlockSpec((HALO, d), lambda b: (seq_of(b) * tiles * blk, 0)),
                      _vec_spec(d), w_spec, _vec_spec(d)],
            out_specs=pl.BlockSpec((N_META, d), lambda b: (b, 0)),
            out_shape=jax.ShapeDtypeStruct((ROW_TILE, d), F32),
            scratch_shapes=[pltpu.VMEM((N_META + 2 * HALO, d), F32)],
            compiler_params=_params("arbitrary"),
            name="pool_mixer_meta",
        )(h, h, gain, w_pool, scale)
        in_specs, operands = [pl.BlockSpec((ROW_TILE, d), lambda i: (0, 0))], [tail]

    real = lambda i: jnp.minimum(i, n_real_tiles - 1)
    cur_spec = pl.BlockSpec((ROW_TILE, d), lambda i: (real(i), 0))
    prev_spec = pl.BlockSpec((HALO, d), lambda i: (
        jnp.where(real(i) % tiles == 0, meta_blk0 + 2 * (real(i) // tiles) + 1, real(i) * blk - 1), 0))
    next_spec = pl.BlockSpec((HALO, d), lambda i: (jnp.where(real(i) % tiles == tiles - 1, 0, (real(i) + 1) * blk), 0))
    n_out_tiles = n_real_tiles + (1 if with_meta else 0)
    return pl.pallas_call(
        functools.partial(_pool_kernel, seq=seq, n_real_tiles=n_real_tiles, with_tail=with_meta),
        grid=(n_out_tiles,),
        in_specs=[cur_spec, prev_spec, next_spec, _vec_spec(d), w_spec, _vec_spec(d)] + in_specs,
        out_specs=pl.BlockSpec((ROW_TILE, d), lambda i: (i, 0)),
        out_shape=jax.ShapeDtypeStruct((n_out_tiles * ROW_TILE, d), F32),
        scratch_shapes=[pltpu.VMEM((ROW_TILE + 2 * HALO, d), F32)],
        compiler_params=_params("parallel"),
        name="pool_mixer",
    )(h, h, h, gain, w_pool, scale, *operands)


def kernel(x_prompt, x_sample, meta_tokens, attn_norm, w_qkv, q_norm, k_norm, w_o, pool_norm, w_pool, pool_scale,
           mlp_norm, w_up, w_down, final_norm):
    b1, seq, d = x_prompt.shape
    b2 = x_sample.shape[0]
    assert x_sample.shape[1:] == (seq, d) and seq % ROW_TILE == 0 and seq % GRID_W == 0
    nseq = b1 + b2
    n_real = nseq * seq
    assert nseq * N_META <= ROW_TILE
    depth = mlp_norm.shape[0]
    t1, t2 = b1 * seq // ROW_TILE, b2 * seq // ROW_TILE

    tail = jnp.concatenate([jnp.tile(meta_tokens.astype(F32), (nseq, 1)),
                            jnp.zeros((ROW_TILE - nseq * N_META, d), F32)], axis=0)
    parts = [(x_prompt.reshape(b1 * seq, d), t1), (x_sample.reshape(b2 * seq, d), t2), (tail, 1)]

    nqk = (N_HEADS + N_KV_HEADS) * HEAD_DIM
    tables = _rope_tables(seq)
    w_qkv = jnp.concatenate([_split_pairs(w_qkv[..., :nqk], N_HEADS + N_KV_HEADS), w_qkv[..., nqk:]], axis=-1)
    w_qkv, w_o, w_pool, w_up, w_down = (w.astype(BF16) for w in (w_qkv, w_o, w_pool, w_up, w_down))
    row2 = lambda a: a.reshape(1, -1).astype(F32)
    q_gain = _split_pairs(q_norm.astype(F32), 1) * (HEAD_DIM ** -0.5 * math.log2(math.e))
    k_gain = _split_pairs(k_norm.astype(F32), 1)

    outs = None
    h = None
    for i in range(depth):
        j = i // 2
        last = i == depth - 1
        if i % 2 == 0:
            q, k, v = _qkv_call(parts, row2(attn_norm[j]), w_qkv, j, row2(q_gain[j]), row2(k_gain[j]), tables, seq, n_real)
            o_real, o_meta = _attn_call(q, k, v, nseq, seq, n_real)
            h = _oproj_call(parts, o_real, o_meta, w_o, j)
        else:
            h = _pool_call(h, row2(pool_norm[j]), w_pool, j, row2(pool_scale[j]), nseq, seq, n_real, with_meta=not last)
        if not last:
            h = _mlp_call(h, row2(mlp_norm[i]), w_up, w_down, i, row2(final_norm), 0, h.shape[0] // ROW_TILE, False)
            parts = [(h, h.shape[0] // ROW_TILE)]
        else:
            mlp = functools.partial(_mlp_call, h, row2(mlp_norm[i]), w_up, w_down, i, row2(final_norm))
            outs = (mlp(0, t1, True).reshape(b1, seq, d), mlp(t1, t2, True).reshape(b2, seq, d))
    return outs
```

```python
import functools
import math

import numpy as np
import jax
import jax.numpy as jnp
from jax import lax
from jax.experimental import pallas as pl
from jax.experimental.pallas import tpu as pltpu

N_HEADS = 16
N_KV_HEADS = 4
Q_PER_KV = N_HEADS // N_KV_HEADS
HEAD_DIM = 128
ROPE_THETA = 10000.0
GRID_W = 64
N_META = 16
POOL_WINDOWS = (2, 4, 8, 16)
EPS = 1e-6

LANES = 128
F32_SUBLANES = 8
HALO = max(POOL_WINDOWS) // 2
ROW_TILE = 512
FF_TILE = 1024
Q_TILE = 512
MXU_COLS = 256
VMEM_LIMIT_BYTES = 56 * 1024 * 1024

assert HALO == F32_SUBLANES and N_META == 2 * HALO and HEAD_DIM == LANES

BF16 = jnp.bfloat16
F32 = jnp.float32


def _params(*semantics):
    return pltpu.CompilerParams(dimension_semantics=semantics, vmem_limit_bytes=VMEM_LIMIT_BYTES)


def _rmsnorm(x, gain):
    ms = jnp.mean(x * x, axis=-1, keepdims=True)
    return x * lax.rsqrt(ms + EPS) * gain


def _layer_resident(w, layer):
    zeros = (0,) * (w.ndim - 1)
    return pl.BlockSpec((None,) + w.shape[1:], lambda *_: (layer,) + zeros, pipeline_mode=pl.Buffered(1))


def _vec_spec(width):
    return pl.BlockSpec((1, width), lambda *_: (0, 0))


def _row_part_specs(parts, d):
    specs, start = [], 0
    for _, n in parts:
        index_map = functools.partial(lambda i, *_, start, n: (jnp.clip(i - start, 0, n - 1), 0), start=start, n=n)
        mode = dict(pipeline_mode=pl.Buffered(1)) if n == 1 and len(parts) > 1 else {}
        specs.append(pl.BlockSpec((ROW_TILE, d), index_map, **mode))
        start += n
    return specs


def _read_row_tile(refs, counts):
    i = pl.program_id(0)
    x = refs[0][...]
    start = 0
    for ref, n in zip(refs[1:], counts[:-1]):
        start += n
        x = jnp.where(i >= start, ref[...], x)
    return x


def _split_pairs(w, n_heads):
    lead = w.shape[:-1]
    return w.reshape(*lead, n_heads, HEAD_DIM // 2, 2).swapaxes(-1, -2).reshape(*lead, n_heads * HEAD_DIM)


def _rope_tables(seq):
    t = np.arange((seq // GRID_W) * GRID_W)
    r = (t // GRID_W).astype(np.float32)
    c = (t % GRID_W).astype(np.float32)
    axis_dim = HEAD_DIM // 2
    inv_freq = (ROPE_THETA ** (-np.arange(0, axis_dim, 2, dtype=np.float32) / axis_dim)).astype(np.float32)
    ang = np.concatenate([r[:, None] * inv_freq[None], c[:, None] * inv_freq[None]], axis=-1)
    cos, sin = np.cos(ang).astype(np.float32), np.sin(ang).astype(np.float32)
    cos = np.concatenate([cos, cos], axis=-1)
    sin = np.concatenate([-sin, sin], axis=-1)
    ident = np.ones((ROW_TILE, HEAD_DIM), np.float32)
    zero = np.zeros((ROW_TILE, HEAD_DIM), np.float32)
    return jnp.asarray(np.concatenate([cos, ident])), jnp.asarray(np.concatenate([sin, zero]))


def _qkv_kernel(*refs, counts):
    n = len(counts)
    gain_ref, w_ref, qg_ref, kg_ref, cos_ref, sin_ref, q_ref, k_ref, v_ref, y_ref = refs[n:]
    i = pl.program_id(0)
    n_tiles = pl.num_programs(0) - 1
    nq, nk = q_ref.shape[1], k_ref.shape[1]

    def finish(col):
        gain = qg_ref[...] if col < nq else kg_ref[...]
        yn = _rmsnorm(y_ref[:, col:col + HEAD_DIM], gain)
        out = (yn * cos_ref[...] + pltpu.roll(yn, HEAD_DIM // 2, 1) * sin_ref[...]).astype(BF16)
        if col < nq:
            q_ref[:, col:col + HEAD_DIM] = out
        else:
            k_ref[:, col - nq:col - nq + HEAD_DIM] = out

    def step(project, finish_previous):
        if project:
            xn = _rmsnorm(_read_row_tile(refs[:n], counts), gain_ref[...]).astype(BF16)
        for c0 in range(0, nq + 2 * nk, MXU_COLS):
            head_cols = range(c0, c0 + MXU_COLS, HEAD_DIM)
            if finish_previous:
                for col in head_cols:
                    if col < nq + nk:
                        finish(col)
            if project:
                y = jnp.dot(xn, w_ref[:, c0:c0 + MXU_COLS], preferred_element_type=F32)
                for col in head_cols:
                    yh = y[:, col - c0:col - c0 + HEAD_DIM]
                    if col < nq + nk:
                        y_ref[:, col:col + HEAD_DIM] = yh
                    else:
                        v_ref[:, col - nq - nk:col - nq - nk + HEAD_DIM] = yh.astype(BF16)

    pl.when(i == 0)(functools.partial(step, True, False))
    pl.when((i > 0) & (i < n_tiles))(functools.partial(step, True, True))
    pl.when(i == n_tiles)(functools.partial(step, False, True))


def _qkv_call(parts, gain, w_qkv, layer, q_gain, k_gain, tables, seq, n_real):
    d = w_qkv.shape[1]
    counts = tuple(n for _, n in parts)
    n_tiles = sum(counts)
    ntok = n_tiles * ROW_TILE
    nq, nk = N_HEADS * HEAD_DIM, N_KV_HEADS * HEAD_DIM
    tiles_per_seq, n_real_tiles = seq // ROW_TILE, n_real // ROW_TILE
    finished = lambda i: jnp.maximum(i - 1, 0)
    row = lambda width, tile: pl.BlockSpec((ROW_TILE, width), lambda i: (tile(i), 0))
    table = pl.BlockSpec((ROW_TILE, HEAD_DIM), lambda i: (
        jnp.where(finished(i) < n_real_tiles, finished(i) % tiles_per_seq, tiles_per_seq), 0))
    return pl.pallas_call(
        functools.partial(_qkv_kernel, counts=counts),
        grid=(n_tiles + 1,),
        in_specs=_row_part_specs(parts, d) + [_vec_spec(d), _layer_resident(w_qkv, layer), _vec_spec(HEAD_DIM),
                                              _vec_spec(HEAD_DIM), table, table],
        out_specs=[row(nq, finished), row(nk, finished), row(nk, lambda i: jnp.minimum(i, n_tiles - 1))],
        out_shape=[jax.ShapeDtypeStruct((ntok, nq), BF16),
                   jax.ShapeDtypeStruct((ntok, nk), BF16),
                   jax.ShapeDtypeStruct((ntok, nk), BF16)],
        scratch_shapes=[pltpu.VMEM((ROW_TILE, nq + nk), F32)],
        compiler_params=_params("arbitrary"),
        name="qkv_proj",
    )(*[a for a, _ in parts], gain, w_qkv, q_gain, k_gain, *tables)


def _attn_kernel(q_ref, qm_ref, k_ref, km_ref, v_ref, vm_ref, zeros_ref, o_ref, om_ref,
                 kpad_ref, vt_ref, vtpad_ref, s_ref, sm_ref, max_ref):
    del zeros_ref
    i = pl.program_id(2)
    q_tiles = pl.num_programs(2) - 1

    def scores(qh):
        contract_last = (((1,), (1,)), ((), ()))
        st = lax.dot_general(k_ref[...], qh, contract_last, preferred_element_type=F32)
        stm = lax.dot_general(kpad_ref[...], qh, contract_last, preferred_element_type=F32)
        key = lax.broadcasted_iota(jnp.int32, stm.shape, 0)
        stm = jnp.where(key < N_META, stm, -jnp.inf)
        return st, stm, jnp.maximum(jnp.max(st, axis=0, keepdims=True), jnp.max(stm, axis=0, keepdims=True))

    def softmax_pv(st, stm, m):
        p, pm = jnp.exp2(st - m), jnp.exp2(stm - m)
        denom = jnp.sum(p, axis=0, keepdims=True) + jnp.sum(pm, axis=0, keepdims=True)
        ot = (jnp.dot(vt_ref[...], p.astype(BF16), preferred_element_type=F32)
              + jnp.dot(vtpad_ref[...], pm.astype(BF16), preferred_element_type=F32))
        return (ot * (1.0 / denom)).T

    def step(score_tile, finish_previous, with_meta):
        for hq in range(Q_PER_KV):
            cols = slice(hq * HEAD_DIM, (hq + 1) * HEAD_DIM)
            if finish_previous:
                o_ref[:, cols] = softmax_pv(s_ref[hq], sm_ref[hq], max_ref[hq]).astype(BF16)
            if score_tile:
                s_ref[hq], sm_ref[hq], max_ref[hq] = scores(q_ref[:, cols])
        if with_meta:
            qm = [qm_ref[:, hq * HEAD_DIM:(hq + 1) * HEAD_DIM] for hq in range(Q_PER_KV)]
            qm.append(jnp.zeros((LANES - Q_PER_KV * N_META, HEAD_DIM), BF16))
            om = softmax_pv(*scores(jnp.concatenate(qm, axis=0)))
            for hq in range(Q_PER_KV):
                om_ref[:, hq * HEAD_DIM:(hq + 1) * HEAD_DIM] = om[hq * N_META:(hq + 1) * N_META].astype(BF16)

    @pl.when(i == 0)
    def _():
        pad = jnp.zeros((LANES - N_META, HEAD_DIM), BF16)
        kpad_ref[0:N_META, :] = km_ref[...]
        kpad_ref[N_META:, :] = pad
        vt_ref[...] = v_ref[...].astype(F32).T.astype(BF16)
        vtpad_ref[...] = jnp.concatenate([vm_ref[...], pad], axis=0).astype(F32).T.astype(BF16)
        step(True, False, False)

    pl.when((i > 0) & (i < q_tiles))(functools.partial(step, True, True, False))
    pl.when(i == q_tiles)(functools.partial(step, False, True, True))


def _attn_call(q, k, v, nseq, seq, n_real):
    gw = Q_PER_KV * HEAD_DIM
    q_tiles = seq // Q_TILE
    meta0 = n_real // N_META
    q_spec = pl.BlockSpec((Q_TILE, gw), lambda b, g, i: (b * q_tiles + jnp.minimum(i, q_tiles - 1), g))
    o_spec = pl.BlockSpec((Q_TILE, gw), lambda b, g, i: (b * q_tiles + jnp.maximum(i - 1, 0), g))
    qm_spec = pl.BlockSpec((N_META, gw), lambda b, g, i: (meta0 + b, g))
    kv_spec = pl.BlockSpec((seq, HEAD_DIM), lambda b, g, i: (b, g))
    kvm_spec = pl.BlockSpec((N_META, HEAD_DIM), lambda b, g, i: (meta0 + b, g))
    om_spec = pl.BlockSpec((N_META, gw), lambda b, g, i: (b, g))
    zeros = jnp.zeros((ROW_TILE, N_HEADS * HEAD_DIM), BF16)
    return pl.pallas_call(
        _attn_kernel,
        grid=(nseq, N_KV_HEADS, q_tiles + 1),
        in_specs=[q_spec, qm_spec, kv_spec, kvm_spec, kv_spec, kvm_spec, pl.BlockSpec(memory_space=pl.ANY)],
        out_specs=[o_spec, om_spec],
        out_shape=[jax.ShapeDtypeStruct((n_real, N_HEADS * HEAD_DIM), BF16),
                   jax.ShapeDtypeStruct(zeros.shape, BF16)],
        scratch_shapes=[pltpu.VMEM((LANES, HEAD_DIM), BF16), pltpu.VMEM((HEAD_DIM, seq), BF16),
                        pltpu.VMEM((HEAD_DIM, LANES), BF16),
                        pltpu.VMEM((Q_PER_KV, seq, Q_TILE), F32), pltpu.VMEM((Q_PER_KV, LANES, Q_TILE), F32),
                        pltpu.VMEM((Q_PER_KV, 1, Q_TILE), F32)],
        input_output_aliases={6: 1},
        compiler_params=_params("parallel", "parallel", "arbitrary"),
        name="attention",
    )(q, q, k, k, v, v, zeros)


def _oproj_kernel(*refs, counts):
    n = len(counts)
    o_ref, om_ref, w_ref, out_ref = refs[n:]
    is_meta_tile = pl.program_id(0) == pl.num_programs(0) - 1
    o = jnp.where(is_meta_tile, om_ref[...], o_ref[...])
    out_ref[...] = _read_row_tile(refs[:n], counts) + jnp.dot(o, w_ref[...], preferred_element_type=F32)


def _oproj_call(parts, o_real, o_meta, w_o, layer):
    d = w_o.shape[2]
    counts = tuple(n for _, n in parts)
    n_tiles = sum(counts)
    n_real_tiles = o_real.shape[0] // ROW_TILE
    o_spec = pl.BlockSpec((ROW_TILE, o_real.shape[1]), lambda i: (jnp.minimum(i, n_real_tiles - 1), 0))
    om_spec = pl.BlockSpec((ROW_TILE, o_real.shape[1]), lambda i: (0, 0))
    return pl.pallas_call(
        functools.partial(_oproj_kernel, counts=counts),
        grid=(n_tiles,),
        in_specs=_row_part_specs(parts, d) + [o_spec, om_spec, _layer_resident(w_o, layer)],
        out_specs=pl.BlockSpec((ROW_TILE, d), lambda i: (i, 0)),
        out_shape=jax.ShapeDtypeStruct((n_tiles * ROW_TILE, d), F32),
        compiler_params=_params("parallel"),
        name="attn_out_proj",
    )(*[a for a, _ in parts], o_real, o_meta, w_o)


def _mlp_kernel(h_ref, gain_ref, wu_ref, wd_ref, fgain_ref, out_ref, xn_ref, *, final_norm):
    k = pl.program_id(1)

    @pl.when(k == 0)
    def _():
        x = h_ref[...]
        xn_ref[...] = _rmsnorm(x, gain_ref[...]).astype(BF16)
        out_ref[...] = x

    u = jnp.dot(xn_ref[...], wu_ref[...], preferred_element_type=F32)
    a = jnp.square(jnp.maximum(u, 0.0)).astype(BF16)
    out_ref[...] += jnp.dot(a, wd_ref[...], preferred_element_type=F32)

    if final_norm:
        @pl.when(k == pl.num_programs(1) - 1)
        def _():
            out_ref[...] = _rmsnorm(out_ref[...], fgain_ref[...])


def _mlp_call(h, gain, w_up, w_down, layer, final_gain, first_tile, n_tiles, final_norm):
    _, d, dff = w_up.shape
    return pl.pallas_call(
        functools.partial(_mlp_kernel, final_norm=final_norm),
        grid=(n_tiles, dff // FF_TILE),
        in_specs=[pl.BlockSpec((ROW_TILE, d), lambda i, k: (first_tile + i, 0)), _vec_spec(d),
                  pl.BlockSpec((None, d, FF_TILE), lambda i, k: (layer, 0, k)),
                  pl.BlockSpec((None, FF_TILE, d), lambda i, k: (layer, k, 0)), _vec_spec(d)],
        out_specs=pl.BlockSpec((ROW_TILE, d), lambda i, k: (i, 0)),
        out_shape=jax.ShapeDtypeStruct((n_tiles * ROW_TILE, d), F32),
        scratch_shapes=[pltpu.VMEM((ROW_TILE, d), BF16)],
        compiler_params=_params("parallel", "arbitrary"),
        name="mlp_final" if final_norm else "mlp",
    )(h, gain, w_up, w_down, final_gain)


def _band_matrices(n, kp):
    r = lax.broadcasted_iota(jnp.int32, (n, kp), 0)
    c = lax.broadcasted_iota(jnp.int32, (n, kp), 1)
    off = c - r - HALO
    return [jnp.where((off >= -(w // 2)) & (off < w // 2), 1.0, 0.0).astype(BF16) for w in POOL_WINDOWS]


def _extended_rows(prev, cur, nxt, kp):
    pad = jnp.zeros((kp - cur.shape[0] - 2 * HALO, cur.shape[1]), F32)
    return jnp.concatenate([prev, cur, nxt, pad], axis=0).astype(BF16)


def _pool_rows(x, xn, xe_ref, band_ref, first_pos, seq_len, w_ref, scale_ref):
    n = x.shape[0]
    gdim = w_ref.shape[1]
    pos = first_pos + lax.broadcasted_iota(jnp.int32, (n, LANES), 0)
    outs = []
    for g, w in enumerate(POOL_WINDOWS):
        cols = slice(g * gdim, (g + 1) * gdim)
        window_sum = jnp.dot(band_ref[g], xe_ref[:, cols], preferred_element_type=F32)
        cnt = (jnp.minimum(pos + w // 2, seq_len) - jnp.maximum(pos - w // 2, 0)).astype(F32)
        inv = jnp.tile(1.0 / cnt, (1, gdim // LANES))
        mixed = (window_sum * inv - xn[:, cols]).astype(BF16)
        y = jnp.dot(mixed, w_ref[g], preferred_element_type=F32)
        outs.append(x[:, cols] + y * scale_ref[:, cols])
    return jnp.concatenate(outs, axis=-1)


def _pool_kernel(*refs, seq, n_real_tiles, with_tail):
    cur_ref, prev_ref, next_ref, gain_ref, w_ref, scale_ref = refs[:6]
    out_ref, xe_ref, band_ref = refs[-3:]
    i = pl.program_id(0)
    tiles = seq // ROW_TILE
    j = i % tiles

    @pl.when(i == 0)
    def _():
        for g, band in enumerate(_band_matrices(ROW_TILE, xe_ref.shape[0])):
            band_ref[g] = band

    @pl.when(i < n_real_tiles)
    def _():
        gain = gain_ref[...]
        x = cur_ref[...]
        xn = _rmsnorm(x, gain)
        nxt = jnp.where(j == tiles - 1, 0.0, _rmsnorm(next_ref[...], gain))
        xe_ref[...] = _extended_rows(_rmsnorm(prev_ref[...], gain), xn, nxt, xe_ref.shape[0])
        out_ref[...] = _pool_rows(x, xn, xe_ref, band_ref, N_META + j * ROW_TILE, N_META + seq, w_ref, scale_ref)

    if with_tail:
        @pl.when(i == n_real_tiles)
        def _():
            out_ref[...] = refs[6][...]


def _pool_meta_kernel(cur_ref, next_ref, gain_ref, w_ref, scale_ref, out_ref, xe_ref, band_ref, *, nseq, seq):
    b = pl.program_id(0)

    @pl.when(b == 0)
    def _():
        for g, band in enumerate(_band_matrices(N_META, xe_ref.shape[0])):
            band_ref[g] = band

    @pl.when(b < nseq)
    def _():
        gain = gain_ref[...]
        x = cur_ref[...]
        xn = _rmsnorm(x, gain)
        before = jnp.zeros((HALO, x.shape[1]), F32)
        xe_ref[...] = _extended_rows(before, xn, _rmsnorm(next_ref[...], gain), xe_ref.shape[0])
        out_ref[...] = _pool_rows(x, xn, xe_ref, band_ref, 0, N_META + seq, w_ref, scale_ref)

    @pl.when(b >= nseq)
    def _():
        out_ref[...] = jnp.zeros(out_ref.shape, F32)


def _pool_call(h, gain, w_pool, layer, scale, nseq, seq, n_real, with_meta):
    d = h.shape[1]
    tiles = seq // ROW_TILE
    n_real_tiles = n_real // ROW_TILE
    blk = ROW_TILE // HALO
    meta_blk0 = n_real // HALO
    w_spec = _layer_resident(w_pool, layer)
    in_specs, operands = [], []
    if with_meta:
        meta0 = n_real // N_META
        seq_of = lambda b: jnp.minimum(b, nseq - 1)
        tail = pl.pallas_call(
            functools.partial(_pool_meta_kernel, nseq=nseq, seq=seq),
            grid=(ROW_TILE // N_META,),
            in_specs=[pl.BlockSpec((N_META, d), lambda b: (meta0 + seq_of(b), 0)),
                      pl.BlockSpec((HALO, d), lambda b: (seq_of(b) * tiles * blk, 0)),
                      _vec_spec(d), w_spec, _vec_spec(d)],
            out_specs=pl.BlockSpec((N_META, d), lambda b: (b, 0)),
            out_shape=jax.ShapeDtypeStruct((ROW_TILE, d), F32),
            scratch_shapes=[pltpu.VMEM((LANES, d), BF16), pltpu.VMEM((len(POOL_WINDOWS), N_META, LANES), BF16)],
            compiler_params=_params("arbitrary"),
            name="pool_mixer_meta",
        )(h, h, gain, w_pool, scale)
        in_specs, operands = [pl.BlockSpec((ROW_TILE, d), lambda i: (0, 0))], [tail]

    real = lambda i: jnp.minimum(i, n_real_tiles - 1)
    cur_spec = pl.BlockSpec((ROW_TILE, d), lambda i: (real(i), 0))
    prev_spec = pl.BlockSpec((HALO, d), lambda i: (
        jnp.where(real(i) % tiles == 0, meta_blk0 + 2 * (real(i) // tiles) + 1, real(i) * blk - 1), 0))
    next_spec = pl.BlockSpec((HALO, d), lambda i: (jnp.where(real(i) % tiles == tiles - 1, 0, (real(i) + 1) * blk), 0))
    n_out_tiles = n_real_tiles + (1 if with_meta else 0)
    ext_rows = -(-(ROW_TILE + 2 * HALO) // LANES) * LANES
    return pl.pallas_call(
        functools.partial(_pool_kernel, seq=seq, n_real_tiles=n_real_tiles, with_tail=with_meta),
        grid=(n_out_tiles,),
        in_specs=[cur_spec, prev_spec, next_spec, _vec_spec(d), w_spec, _vec_spec(d)] + in_specs,
        out_specs=pl.BlockSpec((ROW_TILE, d), lambda i: (i, 0)),
        out_shape=jax.ShapeDtypeStruct((n_out_tiles * ROW_TILE, d), F32),
        scratch_shapes=[pltpu.VMEM((ext_rows, d), BF16), pltpu.VMEM((len(POOL_WINDOWS), ROW_TILE, ext_rows), BF16)],
        compiler_params=_params("arbitrary"),
        name="pool_mixer",
    )(h, h, h, gain, w_pool, scale, *operands)


def kernel(x_prompt, x_sample, meta_tokens, attn_norm, w_qkv, q_norm, k_norm, w_o, pool_norm, w_pool, pool_scale,
           mlp_norm, w_up, w_down, final_norm):
    b1, seq, d = x_prompt.shape
    b2 = x_sample.shape[0]
    assert x_sample.shape[1:] == (seq, d) and seq % ROW_TILE == 0 and seq % GRID_W == 0
    nseq = b1 + b2
    n_real = nseq * seq
    assert nseq * N_META <= ROW_TILE
    depth = mlp_norm.shape[0]
    t1, t2 = b1 * seq // ROW_TILE, b2 * seq // ROW_TILE

    tail = jnp.concatenate([jnp.tile(meta_tokens.astype(F32), (nseq, 1)),
                            jnp.zeros((ROW_TILE - nseq * N_META, d), F32)], axis=0)
    parts = [(x_prompt.reshape(b1 * seq, d), t1), (x_sample.reshape(b2 * seq, d), t2), (tail, 1)]

    nqk = (N_HEADS + N_KV_HEADS) * HEAD_DIM
    tables = _rope_tables(seq)
    w_qkv = jnp.concatenate([_split_pairs(w_qkv[..., :nqk], N_HEADS + N_KV_HEADS), w_qkv[..., nqk:]], axis=-1)
    w_qkv, w_o, w_pool, w_up, w_down = (w.astype(BF16) for w in (w_qkv, w_o, w_pool, w_up, w_down))
    row2 = lambda a: a.reshape(1, -1).astype(F32)
    q_gain = _split_pairs(q_norm.astype(F32), 1) * (HEAD_DIM ** -0.5 * math.log2(math.e))
    k_gain = _split_pairs(k_norm.astype(F32), 1)

    outs = None
    h = None
    for i in range(depth):
        j = i // 2
        last = i == depth - 1
        if i % 2 == 0:
            q, k, v = _qkv_call(parts, row2(attn_norm[j]), w_qkv, j, row2(q_gain[j]), row2(k_gain[j]), tables, seq, n_real)
            o_real, o_meta = _attn_call(q, k, v, nseq, seq, n_real)
            h = _oproj_call(parts, o_real, o_meta, w_o, j)
        else:
            h = _pool_call(h, row2(pool_norm[j]), w_pool, j, row2(pool_scale[j]), nseq, seq, n_real, with_meta=not last)
        if not last:
            h = _mlp_call(h, row2(mlp_norm[i]), w_up, w_down, i, row2(final_norm), 0, h.shape[0] // ROW_TILE, False)
            parts = [(h, h.shape[0] // ROW_TILE)]
        else:
            mlp = functools.partial(_mlp_call, h, row2(mlp_norm[i]), w_up, w_down, i, row2(final_norm))
            outs = (mlp(0, t1, True).reshape(b1, seq, d), mlp(t1, t2, True).reshape(b2, seq, d))
    return outs
```

```python
import functools
import math

import numpy as np
import jax
import jax.numpy as jnp
from jax import lax
from jax.experimental import pallas as pl
from jax.experimental.pallas import tpu as pltpu

N_HEADS = 16
N_KV_HEADS = 4
Q_PER_KV = N_HEADS // N_KV_HEADS
HEAD_DIM = 128
ROPE_THETA = 10000.0
GRID_W = 64
N_META = 16
POOL_WINDOWS = (2, 4, 8, 16)
EPS = 1e-6

LANES = 128
F32_SUBLANES = 8
HALO = max(POOL_WINDOWS) // 2
ROW_TILE = 512
FF_TILE = 1024
Q_TILE = 512
MXU_COLS = 256
VMEM_LIMIT_BYTES = 56 * 1024 * 1024

assert HALO == F32_SUBLANES and N_META == 2 * HALO and HEAD_DIM == LANES

BF16 = jnp.bfloat16
F32 = jnp.float32


def _params(*semantics):
    return pltpu.CompilerParams(dimension_semantics=semantics, vmem_limit_bytes=VMEM_LIMIT_BYTES)


def _rmsnorm(x, gain):
    ms = jnp.mean(x * x, axis=-1, keepdims=True)
    return x * lax.rsqrt(ms + EPS) * gain


def _layer_resident(w, layer):
    zeros = (0,) * (w.ndim - 1)
    return pl.BlockSpec((None,) + w.shape[1:], lambda *_: (layer,) + zeros, pipeline_mode=pl.Buffered(1))


def _vec_spec(width):
    return pl.BlockSpec((1, width), lambda *_: (0, 0))


def _row_part_specs(parts, d):
    specs, start = [], 0
    for _, n in parts:
        index_map = functools.partial(lambda i, *_, start, n: (jnp.clip(i - start, 0, n - 1), 0), start=start, n=n)
        mode = dict(pipeline_mode=pl.Buffered(1)) if n == 1 and len(parts) > 1 else {}
        specs.append(pl.BlockSpec((ROW_TILE, d), index_map, **mode))
        start += n
    return specs


def _read_row_tile(refs, counts):
    i = pl.program_id(0)
    x = refs[0][...]
    start = 0
    for ref, n in zip(refs[1:], counts[:-1]):
        start += n
        x = jnp.where(i >= start, ref[...], x)
    return x


def _split_pairs(w, n_heads):
    lead = w.shape[:-1]
    return w.reshape(*lead, n_heads, HEAD_DIM // 2, 2).swapaxes(-1, -2).reshape(*lead, n_heads * HEAD_DIM)


def _rope_tables(seq):
    t = np.arange((seq // GRID_W) * GRID_W)
    r = (t // GRID_W).astype(np.float32)
    c = (t % GRID_W).astype(np.float32)
    axis_dim = HEAD_DIM // 2
    inv_freq = (ROPE_THETA ** (-np.arange(0, axis_dim, 2, dtype=np.float32) / axis_dim)).astype(np.float32)
    ang = np.concatenate([r[:, None] * inv_freq[None], c[:, None] * inv_freq[None]], axis=-1)
    cos, sin = np.cos(ang).astype(np.float32), np.sin(ang).astype(np.float32)
    cos = np.concatenate([cos, cos], axis=-1)
    sin = np.concatenate([-sin, sin], axis=-1)
    ident = np.ones((ROW_TILE, HEAD_DIM), np.float32)
    zero = np.zeros((ROW_TILE, HEAD_DIM), np.float32)
    return jnp.asarray(np.concatenate([cos, ident])), jnp.asarray(np.concatenate([sin, zero]))


def _qkv_kernel(*refs, counts):
    n = len(counts)
    gain_ref, w_ref, qg_ref, kg_ref, cos_ref, sin_ref, q_ref, k_ref, v_ref, y_ref = refs[n:]
    i = pl.program_id(0)
    n_tiles = pl.num_programs(0) - 1
    nq, nk = q_ref.shape[1], k_ref.shape[1]

    def finish(col):
        gain = qg_ref[...] if col < nq else kg_ref[...]
        yn = _rmsnorm(y_ref[:, col:col + HEAD_DIM], gain)
        out = (yn * cos_ref[...] + pltpu.roll(yn, HEAD_DIM // 2, 1) * sin_ref[...]).astype(BF16)
        if col < nq:
            q_ref[:, col:col + HEAD_DIM] = out
        else:
            k_ref[:, col - nq:col - nq + HEAD_DIM] = out

    def step(project, finish_previous):
        if project:
            xn = _rmsnorm(_read_row_tile(refs[:n], counts), gain_ref[...]).astype(BF16)
        for c0 in range(0, nq + 2 * nk, MXU_COLS):
            head_cols = range(c0, c0 + MXU_COLS, HEAD_DIM)
            if finish_previous:
                for col in head_cols:
                    if col < nq + nk:
                        finish(col)
            if project:
                y = jnp.dot(xn, w_ref[:, c0:c0 + MXU_COLS], preferred_element_type=F32)
                for col in head_cols:
                    yh = y[:, col - c0:col - c0 + HEAD_DIM]
                    if col < nq + nk:
                        y_ref[:, col:col + HEAD_DIM] = yh
                    else:
                        v_ref[:, col - nq - nk:col - nq - nk + HEAD_DIM] = yh.astype(BF16)

    pl.when(i == 0)(functools.partial(step, True, False))
    pl.when((i > 0) & (i < n_tiles))(functools.partial(step, True, True))
    pl.when(i == n_tiles)(functools.partial(step, False, True))


def _qkv_call(parts, gain, w_qkv, layer, q_gain, k_gain, tables, seq, n_real):
    d = w_qkv.shape[1]
    counts = tuple(n for _, n in parts)
    n_tiles = sum(counts)
    ntok = n_tiles * ROW_TILE
    nq, nk = N_HEADS * HEAD_DIM, N_KV_HEADS * HEAD_DIM
    tiles_per_seq, n_real_tiles = seq // ROW_TILE, n_real // ROW_TILE
    finished = lambda i: jnp.maximum(i - 1, 0)
    row = lambda width, tile: pl.BlockSpec((ROW_TILE, width), lambda i: (tile(i), 0))
    table = pl.BlockSpec((ROW_TILE, HEAD_DIM), lambda i: (
        jnp.where(finished(i) < n_real_tiles, finished(i) % tiles_per_seq, tiles_per_seq), 0))
    return pl.pallas_call(
        functools.partial(_qkv_kernel, counts=counts),
        grid=(n_tiles + 1,),
        in_specs=_row_part_specs(parts, d) + [_vec_spec(d), _layer_resident(w_qkv, layer), _vec_spec(HEAD_DIM),
                                              _vec_spec(HEAD_DIM), table, table],
        out_specs=[row(nq, finished), row(nk, finished), row(nk, lambda i: jnp.minimum(i, n_tiles - 1))],
        out_shape=[jax.ShapeDtypeStruct((ntok, nq), BF16),
                   jax.ShapeDtypeStruct((ntok, nk), BF16),
                   jax.ShapeDtypeStruct((ntok, nk), BF16)],
        scratch_shapes=[pltpu.VMEM((ROW_TILE, nq + nk), F32)],
        compiler_params=_params("arbitrary"),
        name="qkv_proj",
    )(*[a for a, _ in parts], gain, w_qkv, q_gain, k_gain, *tables)


def _attn_kernel(q_ref, qm_ref, k_ref, km_ref, v_ref, vm_ref, zeros_ref, o_ref, om_ref, vt_ref, vtpad_ref):
    del zeros_ref
    first_q_tile = pl.program_id(2) == 0

    @pl.when(first_q_tile)
    def _():
        pad = jnp.zeros((LANES - N_META, HEAD_DIM), BF16)
        vt_ref[...] = v_ref[...].astype(F32).T.astype(BF16)
        vtpad_ref[...] = jnp.concatenate([vm_ref[...], pad], axis=0).astype(F32).T.astype(BF16)

    def scores(qh):
        contract_last = (((1,), (1,)), ((), ()))
        st = lax.dot_general(k_ref[...], qh, contract_last, preferred_element_type=F32)
        stm = lax.dot_general(km_ref[...], qh, contract_last, preferred_element_type=F32)
        return st, stm

    def softmax_pv(st, stm):
        m = jnp.maximum(jnp.max(st, axis=0, keepdims=True), jnp.max(stm, axis=0, keepdims=True))
        p, pm = jnp.exp2(st - m), jnp.exp2(stm - m)
        denom = jnp.sum(p, axis=0, keepdims=True) + jnp.sum(pm, axis=0, keepdims=True)
        pm = jnp.concatenate([pm.astype(BF16), jnp.zeros((LANES - N_META, pm.shape[1]), BF16)], axis=0)
        ot = (jnp.dot(vt_ref[...], p.astype(BF16), preferred_element_type=F32)
              + jnp.dot(vtpad_ref[...], pm, preferred_element_type=F32))
        return (ot * (1.0 / denom)).T

    def attend(qh):
        return softmax_pv(*scores(qh))

    head_cols = [slice(hq * HEAD_DIM, (hq + 1) * HEAD_DIM) for hq in range(Q_PER_KV)]
    head_scores = [scores(q_ref[:, cols]) for cols in head_cols]
    for cols, s in zip(head_cols, head_scores):
        o_ref[:, cols] = softmax_pv(*s).astype(BF16)

    @pl.when(first_q_tile)
    def _():
        qm = [qm_ref[:, hq * HEAD_DIM:(hq + 1) * HEAD_DIM] for hq in range(Q_PER_KV)]
        qm.append(jnp.zeros((LANES - Q_PER_KV * N_META, HEAD_DIM), BF16))
        om = attend(jnp.concatenate(qm, axis=0))
        for hq in range(Q_PER_KV):
            om_ref[:, hq * HEAD_DIM:(hq + 1) * HEAD_DIM] = om[hq * N_META:(hq + 1) * N_META].astype(BF16)


def _attn_call(q, k, v, nseq, seq, n_real):
    gw = Q_PER_KV * HEAD_DIM
    q_tiles = seq // Q_TILE
    meta0 = n_real // N_META
    q_spec = pl.BlockSpec((Q_TILE, gw), lambda b, g, i: (b * q_tiles + i, g))
    qm_spec = pl.BlockSpec((N_META, gw), lambda b, g, i: (meta0 + b, g))
    kv_spec = pl.BlockSpec((seq, HEAD_DIM), lambda b, g, i: (b, g))
    kvm_spec = pl.BlockSpec((N_META, HEAD_DIM), lambda b, g, i: (meta0 + b, g))
    om_spec = pl.BlockSpec((N_META, gw), lambda b, g, i: (b, g))
    zeros = jnp.zeros((ROW_TILE, N_HEADS * HEAD_DIM), BF16)
    return pl.pallas_call(
        _attn_kernel,
        grid=(nseq, N_KV_HEADS, q_tiles),
        in_specs=[q_spec, qm_spec, kv_spec, kvm_spec, kv_spec, kvm_spec, pl.BlockSpec(memory_space=pl.ANY)],
        out_specs=[q_spec, om_spec],
        out_shape=[jax.ShapeDtypeStruct((n_real, N_HEADS * HEAD_DIM), BF16),
                   jax.ShapeDtypeStruct(zeros.shape, BF16)],
        scratch_shapes=[pltpu.VMEM((HEAD_DIM, seq), BF16), pltpu.VMEM((HEAD_DIM, LANES), BF16)],
        input_output_aliases={6: 1},
        compiler_params=_params("parallel", "parallel", "arbitrary"),
        name="attention",
    )(q, q, k, k, v, v, zeros)


def _oproj_kernel(*refs, counts):
    n = len(counts)
    o_ref, om_ref, w_ref, out_ref = refs[n:]
    is_meta_tile = pl.program_id(0) == pl.num_programs(0) - 1
    o = jnp.where(is_meta_tile, om_ref[...], o_ref[...])
    out_ref[...] = _read_row_tile(refs[:n], counts) + jnp.dot(o, w_ref[...], preferred_element_type=F32)


def _oproj_call(parts, o_real, o_meta, w_o, layer):
    d = w_o.shape[2]
    counts = tuple(n for _, n in parts)
    n_tiles = sum(counts)
    n_real_tiles = o_real.shape[0] // ROW_TILE
    o_spec = pl.BlockSpec((ROW_TILE, o_real.shape[1]), lambda i: (jnp.minimum(i, n_real_tiles - 1), 0))
    om_spec = pl.BlockSpec((ROW_TILE, o_real.shape[1]), lambda i: (0, 0))
    return pl.pallas_call(
        functools.partial(_oproj_kernel, counts=counts),
        grid=(n_tiles,),
        in_specs=_row_part_specs(parts, d) + [o_spec, om_spec, _layer_resident(w_o, layer)],
        out_specs=pl.BlockSpec((ROW_TILE, d), lambda i: (i, 0)),
        out_shape=jax.ShapeDtypeStruct((n_tiles * ROW_TILE, d), F32),
        compiler_params=_params("parallel"),
        name="attn_out_proj",
    )(*[a for a, _ in parts], o_real, o_meta, w_o)


def _mlp_kernel(h_ref, gain_ref, wu_ref, wd_ref, fgain_ref, out_ref, xn_ref, *, final_norm):
    k = pl.program_id(1)

    @pl.when(k == 0)
    def _():
        x = h_ref[...]
        xn_ref[...] = _rmsnorm(x, gain_ref[...]).astype(BF16)
        out_ref[...] = x

    u = jnp.dot(xn_ref[...], wu_ref[...], preferred_element_type=F32)
    a = jnp.square(jnp.maximum(u, 0.0)).astype(BF16)
    out_ref[...] += jnp.dot(a, wd_ref[...], preferred_element_type=F32)

    if final_norm:
        @pl.when(k == pl.num_programs(1) - 1)
        def _():
            out_ref[...] = _rmsnorm(out_ref[...], fgain_ref[...])


def _mlp_call(h, gain, w_up, w_down, layer, final_gain, first_tile, n_tiles, final_norm):
    _, d, dff = w_up.shape
    return pl.pallas_call(
        functools.partial(_mlp_kernel, final_norm=final_norm),
        grid=(n_tiles, dff // FF_TILE),
        in_specs=[pl.BlockSpec((ROW_TILE, d), lambda i, k: (first_tile + i, 0)), _vec_spec(d),
                  pl.BlockSpec((None, d, FF_TILE), lambda i, k: (layer, 0, k)),
                  pl.BlockSpec((None, FF_TILE, d), lambda i, k: (layer, k, 0)), _vec_spec(d)],
        out_specs=pl.BlockSpec((ROW_TILE, d), lambda i, k: (i, 0)),
        out_shape=jax.ShapeDtypeStruct((n_tiles * ROW_TILE, d), F32),
        scratch_shapes=[pltpu.VMEM((ROW_TILE, d), BF16)],
        compiler_params=_params("parallel", "arbitrary"),
        name="mlp_final" if final_norm else "mlp",
    )(h, gain, w_up, w_down, final_gain)


def _band_matrices(n, kp):
    r = lax.broadcasted_iota(jnp.int32, (n, kp), 0)
    c = lax.broadcasted_iota(jnp.int32, (n, kp), 1)
    off = c - r - HALO
    return [jnp.where((off >= -(w // 2)) & (off < w // 2), 1.0, 0.0).astype(BF16) for w in POOL_WINDOWS]


def _extended_rows(prev, cur, nxt, kp):
    pad = jnp.zeros((kp - cur.shape[0] - 2 * HALO, cur.shape[1]), F32)
    return jnp.concatenate([prev, cur, nxt, pad], axis=0).astype(BF16)


def _pool_rows(x, xn, xe_ref, band_ref, first_pos, seq_len, w_ref, scale_ref):
    n = x.shape[0]
    gdim = w_ref.shape[1]
    pos = first_pos + lax.broadcasted_iota(jnp.int32, (n, LANES), 0)
    outs = []
    for g, w in enumerate(POOL_WINDOWS):
        cols = slice(g * gdim, (g + 1) * gdim)
        window_sum = jnp.dot(band_ref[g], xe_ref[:, cols], preferred_element_type=F32)
        cnt = (jnp.minimum(pos + w // 2, seq_len) - jnp.maximum(pos - w // 2, 0)).astype(F32)
        inv = jnp.tile(1.0 / cnt, (1, gdim // LANES))
        mixed = (window_sum * inv - xn[:, cols]).astype(BF16)
        y = jnp.dot(mixed, w_ref[g], preferred_element_type=F32)
        outs.append(x[:, cols] + y * scale_ref[:, cols])
    return jnp.concatenate(outs, axis=-1)


def _pool_kernel(*refs, seq, n_real_tiles, with_tail):
    cur_ref, prev_ref, next_ref, gain_ref, w_ref, scale_ref = refs[:6]
    out_ref, xe_ref, band_ref = refs[-3:]
    i = pl.program_id(0)
    tiles = seq // ROW_TILE
    j = i % tiles

    @pl.when(i == 0)
    def _():
        for g, band in enumerate(_band_matrices(ROW_TILE, xe_ref.shape[0])):
            band_ref[g] = band

    @pl.when(i < n_real_tiles)
    def _():
        gain = gain_ref[...]
        x = cur_ref[...]
        xn = _rmsnorm(x, gain)
        nxt = jnp.where(j == tiles - 1, 0.0, _rmsnorm(next_ref[...], gain))
        xe_ref[...] = _extended_rows(_rmsnorm(prev_ref[...], gain), xn, nxt, xe_ref.shape[0])
        out_ref[...] = _pool_rows(x, xn, xe_ref, band_ref, N_META + j * ROW_TILE, N_META + seq, w_ref, scale_ref)

    if with_tail:
        @pl.when(i == n_real_tiles)
        def _():
            out_ref[...] = refs[6][...]


def _pool_meta_kernel(cur_ref, next_ref, gain_ref, w_ref, scale_ref, out_ref, xe_ref, band_ref, *, nseq, seq):
    b = pl.program_id(0)

    @pl.when(b == 0)
    def _():
        for g, band in enumerate(_band_matrices(N_META, xe_ref.shape[0])):
            band_ref[g] = band

    @pl.when(b < nseq)
    def _():
        gain = gain_ref[...]
        x = cur_ref[...]
        xn = _rmsnorm(x, gain)
        before = jnp.zeros((HALO, x.shape[1]), F32)
        xe_ref[...] = _extended_rows(before, xn, _rmsnorm(next_ref[...], gain), xe_ref.shape[0])
        out_ref[...] = _pool_rows(x, xn, xe_ref, band_ref, 0, N_META + seq, w_ref, scale_ref)

    @pl.when(b >= nseq)
    def _():
        out_ref[...] = jnp.zeros(out_ref.shape, F32)


def _pool_call(h, gain, w_pool, layer, scale, nseq, seq, n_real, with_meta):
    d = h.shape[1]
    tiles = seq // ROW_TILE
    n_real_tiles = n_real // ROW_TILE
    blk = ROW_TILE // HALO
    meta_blk0 = n_real // HALO
    w_spec = _layer_resident(w_pool, layer)
    in_specs, operands = [], []
    if with_meta:
        meta0 = n_real // N_META
        seq_of = lambda b: jnp.minimum(b, nseq - 1)
        tail = pl.pallas_call(
            functools.partial(_pool_meta_kernel, nseq=nseq, seq=seq),
            grid=(ROW_TILE // N_META,),
            in_specs=[pl.BlockSpec((N_META, d), lambda b: (meta0 + seq_of(b), 0)),
                      pl.BlockSpec((HALO, d), lambda b: (seq_of(b) * tiles * blk, 0)),
                      _vec_spec(d), w_spec, _vec_spec(d)],
            out_specs=pl.BlockSpec((N_META, d), lambda b: (b, 0)),
            out_shape=jax.ShapeDtypeStruct((ROW_TILE, d), F32),
            scratch_shapes=[pltpu.VMEM((LANES, d), BF16), pltpu.VMEM((len(POOL_WINDOWS), N_META, LANES), BF16)],
            compiler_params=_params("arbitrary"),
            name="pool_mixer_meta",
        )(h, h, gain, w_pool, scale)
        in_specs, operands = [pl.BlockSpec((ROW_TILE, d), lambda i: (0, 0))], [tail]

    real = lambda i: jnp.minimum(i, n_real_tiles - 1)
    cur_spec = pl.BlockSpec((ROW_TILE, d), lambda i: (real(i), 0))
    prev_spec = pl.BlockSpec((HALO, d), lambda i: (
        jnp.where(real(i) % tiles == 0, meta_blk0 + 2 * (real(i) // tiles) + 1, real(i) * blk - 1), 0))
    next_spec = pl.BlockSpec((HALO, d), lambda i: (jnp.where(real(i) % tiles == tiles - 1, 0, (real(i) + 1) * blk), 0))
    n_out_tiles = n_real_tiles + (1 if with_meta else 0)
    ext_rows = -(-(ROW_TILE + 2 * HALO) // LANES) * LANES
    return pl.pallas_call(
        functools.partial(_pool_kernel, seq=seq, n_real_tiles=n_real_tiles, with_tail=with_meta),
        grid=(n_out_tiles,),
        in_specs=[cur_spec, prev_spec, next_spec, _vec_spec(d), w_spec, _vec_spec(d)] + in_specs,
        out_specs=pl.BlockSpec((ROW_TILE, d), lambda i: (i, 0)),
        out_shape=jax.ShapeDtypeStruct((n_out_tiles * ROW_TILE, d), F32),
        scratch_shapes=[pltpu.VMEM((ext_rows, d), BF16), pltpu.VMEM((len(POOL_WINDOWS), ROW_TILE, ext_rows), BF16)],
        compiler_params=_params("arbitrary"),
        name="pool_mixer",
    )(h, h, h, gain, w_pool, scale, *operands)


def kernel(x_prompt, x_sample, meta_tokens, attn_norm, w_qkv, q_norm, k_norm, w_o, pool_norm, w_pool, pool_scale,
           mlp_norm, w_up, w_down, final_norm):
    b1, seq, d = x_prompt.shape
    b2 = x_sample.shape[0]
    assert x_sample.shape[1:] == (seq, d) and seq % ROW_TILE == 0 and seq % GRID_W == 0
    nseq = b1 + b2
    n_real = nseq * seq
    assert nseq * N_META <= ROW_TILE
    depth = mlp_norm.shape[0]
    t1, t2 = b1 * seq // ROW_TILE, b2 * seq // ROW_TILE

    tail = jnp.concatenate([jnp.tile(meta_tokens.astype(F32), (nseq, 1)),
                            jnp.zeros((ROW_TILE - nseq * N_META, d), F32)], axis=0)
    parts = [(x_prompt.reshape(b1 * seq, d), t1), (x_sample.reshape(b2 * seq, d), t2), (tail, 1)]

    nqk = (N_HEADS + N_KV_HEADS) * HEAD_DIM
    tables = _rope_tables(seq)
    w_qkv = jnp.concatenate([_split_pairs(w_qkv[..., :nqk], N_HEADS + N_KV_HEADS), w_qkv[..., nqk:]], axis=-1)
    w_qkv, w_o, w_pool, w_up, w_down = (w.astype(BF16) for w in (w_qkv, w_o, w_pool, w_up, w_down))
    row2 = lambda a: a.reshape(1, -1).astype(F32)
    q_gain = _split_pairs(q_norm.astype(F32), 1) * (HEAD_DIM ** -0.5 * math.log2(math.e))
    k_gain = _split_pairs(k_norm.astype(F32), 1)

    outs = None
    h = None
    for i in range(depth):
        j = i // 2
        last = i == depth - 1
        if i % 2 == 0:
            q, k, v = _qkv_call(parts, row2(attn_norm[j]), w_qkv, j, row2(q_gain[j]), row2(k_gain[j]), tables, seq, n_real)
            o_real, o_meta = _attn_call(q, k, v, nseq, seq, n_real)
            h = _oproj_call(parts, o_real, o_meta, w_o, j)
        else:
            h = _pool_call(h, row2(pool_norm[j]), w_pool, j, row2(pool_scale[j]), nseq, seq, n_real, with_meta=not last)
        if not last:
            h = _mlp_call(h, row2(mlp_norm[i]), w_up, w_down, i, row2(final_norm), 0, h.shape[0] // ROW_TILE, False)
            parts = [(h, h.shape[0] // ROW_TILE)]
        else:
            mlp = functools.partial(_mlp_call, h, row2(mlp_norm[i]), w_up, w_down, i, row2(final_norm))
            outs = (mlp(0, t1, True).reshape(b1, seq, d), mlp(t1, t2, True).reshape(b2, seq, d))
    return outs
```

```python
import functools
import math

import numpy as np
import jax
import jax.numpy as jnp
from jax import lax
from jax.experimental import pallas as pl
from jax.experimental.pallas import tpu as pltpu

N_HEADS = 16
N_KV_HEADS = 4
Q_PER_KV = N_HEADS // N_KV_HEADS
HEAD_DIM = 128
ROPE_THETA = 10000.0
GRID_W = 64
N_META = 16
POOL_WINDOWS = (2, 4, 8, 16)
EPS = 1e-6

LANES = 128
F32_SUBLANES = 8
HALO = max(POOL_WINDOWS) // 2
ROW_TILE = 512
FF_TILE = 1024
Q_TILE = 512
MXU_COLS = 256
VMEM_LIMIT_BYTES = 56 * 1024 * 1024

assert HALO == F32_SUBLANES and N_META == 2 * HALO and HEAD_DIM == LANES

BF16 = jnp.bfloat16
F32 = jnp.float32


def _params(*semantics):
    return pltpu.CompilerParams(dimension_semantics=semantics, vmem_limit_bytes=VMEM_LIMIT_BYTES)


def _rmsnorm(x, gain):
    ms = jnp.mean(x * x, axis=-1, keepdims=True)
    return x * lax.rsqrt(ms + EPS) * gain


def _layer_resident(w, layer):
    zeros = (0,) * (w.ndim - 1)
    return pl.BlockSpec((None,) + w.shape[1:], lambda *_: (layer,) + zeros, pipeline_mode=pl.Buffered(1))


def _vec_spec(width):
    return pl.BlockSpec((1, width), lambda *_: (0, 0))


def _row_part_specs(parts, d):
    specs, start = [], 0
    for _, n in parts:
        index_map = functools.partial(lambda i, *_, start, n: (jnp.clip(i - start, 0, n - 1), 0), start=start, n=n)
        mode = dict(pipeline_mode=pl.Buffered(1)) if n == 1 and len(parts) > 1 else {}
        specs.append(pl.BlockSpec((ROW_TILE, d), index_map, **mode))
        start += n
    return specs


def _read_row_tile(refs, counts):
    i = pl.program_id(0)
    x = refs[0][...]
    start = 0
    for ref, n in zip(refs[1:], counts[:-1]):
        start += n
        x = jnp.where(i >= start, ref[...], x)
    return x


def _split_pairs(w, n_heads):
    lead = w.shape[:-1]
    return w.reshape(*lead, n_heads, HEAD_DIM // 2, 2).swapaxes(-1, -2).reshape(*lead, n_heads * HEAD_DIM)


def _rope_tables(seq):
    t = np.arange((seq // GRID_W) * GRID_W)
    r = (t // GRID_W).astype(np.float32)
    c = (t % GRID_W).astype(np.float32)
    axis_dim = HEAD_DIM // 2
    inv_freq = (ROPE_THETA ** (-np.arange(0, axis_dim, 2, dtype=np.float32) / axis_dim)).astype(np.float32)
    ang = np.concatenate([r[:, None] * inv_freq[None], c[:, None] * inv_freq[None]], axis=-1)
    cos, sin = np.cos(ang).astype(np.float32), np.sin(ang).astype(np.float32)
    cos = np.concatenate([cos, cos], axis=-1)
    sin = np.concatenate([-sin, sin], axis=-1)
    ident = np.ones((ROW_TILE, HEAD_DIM), np.float32)
    zero = np.zeros((ROW_TILE, HEAD_DIM), np.float32)
    return jnp.asarray(np.concatenate([cos, ident])), jnp.asarray(np.concatenate([sin, zero]))


def _qkv_kernel(*refs, counts):
    n = len(counts)
    gain_ref, w_ref, qg_ref, kg_ref, cos_ref, sin_ref, q_ref, k_ref, v_ref, y_ref = refs[n:]
    i = pl.program_id(0)
    n_tiles = pl.num_programs(0) - 1
    nq, nk = q_ref.shape[1], k_ref.shape[1]

    def finish(col):
        gain = qg_ref[...] if col < nq else kg_ref[...]
        yn = _rmsnorm(y_ref[:, col:col + HEAD_DIM], gain)
        out = (yn * cos_ref[...] + pltpu.roll(yn, HEAD_DIM // 2, 1) * sin_ref[...]).astype(BF16)
        if col < nq:
            q_ref[:, col:col + HEAD_DIM] = out
        else:
            k_ref[:, col - nq:col - nq + HEAD_DIM] = out

    def step(project, finish_previous):
        if project:
            xn = _rmsnorm(_read_row_tile(refs[:n], counts), gain_ref[...]).astype(BF16)
        for c0 in range(0, nq + 2 * nk, MXU_COLS):
            head_cols = range(c0, c0 + MXU_COLS, HEAD_DIM)
            if finish_previous:
                for col in head_cols:
                    if col < nq + nk:
                        finish(col)
            if project:
                y = jnp.dot(xn, w_ref[:, c0:c0 + MXU_COLS], preferred_element_type=F32)
                for col in head_cols:
                    yh = y[:, col - c0:col - c0 + HEAD_DIM]
                    if col < nq + nk:
                        y_ref[:, col:col + HEAD_DIM] = yh
                    else:
                        v_ref[:, col - nq - nk:col - nq - nk + HEAD_DIM] = yh.astype(BF16)

    pl.when(i == 0)(functools.partial(step, True, False))
    pl.when((i > 0) & (i < n_tiles))(functools.partial(step, True, True))
    pl.when(i == n_tiles)(functools.partial(step, False, True))


def _qkv_call(parts, gain, w_qkv, layer, q_gain, k_gain, tables, seq, n_real):
    d = w_qkv.shape[1]
    counts = tuple(n for _, n in parts)
    n_tiles = sum(counts)
    ntok = n_tiles * ROW_TILE
    nq, nk = N_HEADS * HEAD_DIM, N_KV_HEADS * HEAD_DIM
    tiles_per_seq, n_real_tiles = seq // ROW_TILE, n_real // ROW_TILE
    finished = lambda i: jnp.maximum(i - 1, 0)
    row = lambda width, tile: pl.BlockSpec((ROW_TILE, width), lambda i: (tile(i), 0))
    table = pl.BlockSpec((ROW_TILE, HEAD_DIM), lambda i: (
        jnp.where(finished(i) < n_real_tiles, finished(i) % tiles_per_seq, tiles_per_seq), 0))
    return pl.pallas_call(
        functools.partial(_qkv_kernel, counts=counts),
        grid=(n_tiles + 1,),
        in_specs=_row_part_specs(parts, d) + [_vec_spec(d), _layer_resident(w_qkv, layer), _vec_spec(HEAD_DIM),
                                              _vec_spec(HEAD_DIM), table, table],
        out_specs=[row(nq, finished), row(nk, finished), row(nk, lambda i: jnp.minimum(i, n_tiles - 1))],
        out_shape=[jax.ShapeDtypeStruct((ntok, nq), BF16),
                   jax.ShapeDtypeStruct((ntok, nk), BF16),
                   jax.ShapeDtypeStruct((ntok, nk), BF16)],
        scratch_shapes=[pltpu.VMEM((ROW_TILE, nq + nk), F32)],
        compiler_params=_params("arbitrary"),
        name="qkv_proj",
    )(*[a for a, _ in parts], gain, w_qkv, q_gain, k_gain, *tables)


def _attn_kernel(q_ref, qm_ref, k_ref, km_ref, v_ref, vm_ref, zeros_ref, o_ref, om_ref, vt_ref, vtpad_ref):
    del zeros_ref
    first_q_tile = pl.program_id(2) == 0

    @pl.when(first_q_tile)
    def _():
        pad = jnp.zeros((LANES - N_META, HEAD_DIM), BF16)
        vt_ref[...] = v_ref[...].astype(F32).T.astype(BF16)
        vtpad_ref[...] = jnp.concatenate([vm_ref[...], pad], axis=0).astype(F32).T.astype(BF16)

    def scores(qh):
        contract_last = (((1,), (1,)), ((), ()))
        st = lax.dot_general(k_ref[...], qh, contract_last, preferred_element_type=F32)
        stm = lax.dot_general(km_ref[...], qh, contract_last, preferred_element_type=F32)
        return st, stm

    def softmax_pv(st, stm):
        m = jnp.maximum(jnp.max(st, axis=0, keepdims=True), jnp.max(stm, axis=0, keepdims=True))
        p, pm = jnp.exp2(st - m), jnp.exp2(stm - m)
        denom = jnp.sum(p, axis=0, keepdims=True) + jnp.sum(pm, axis=0, keepdims=True)
        pm = jnp.concatenate([pm.astype(BF16), jnp.zeros((LANES - N_META, pm.shape[1]), BF16)], axis=0)
        ot = (jnp.dot(vt_ref[...], p.astype(BF16), preferred_element_type=F32)
              + jnp.dot(vtpad_ref[...], pm, preferred_element_type=F32))
        return (ot * (1.0 / denom)).T

    def attend(qh):
        return softmax_pv(*scores(qh))

    head_cols = [slice(hq * HEAD_DIM, (hq + 1) * HEAD_DIM) for hq in range(Q_PER_KV)]
    head_scores = [scores(q_ref[:, cols]) for cols in head_cols]
    for cols, s in zip(head_cols, head_scores):
        o_ref[:, cols] = softmax_pv(*s).astype(BF16)

    @pl.when(first_q_tile)
    def _():
        qm = [qm_ref[:, hq * HEAD_DIM:(hq + 1) * HEAD_DIM] for hq in range(Q_PER_KV)]
        qm.append(jnp.zeros((LANES - Q_PER_KV * N_META, HEAD_DIM), BF16))
        om = attend(jnp.concatenate(qm, axis=0))
        for hq in range(Q_PER_KV):
            om_ref[:, hq * HEAD_DIM:(hq + 1) * HEAD_DIM] = om[hq * N_META:(hq + 1) * N_META].astype(BF16)


def _attn_call(q, k, v, nseq, seq, n_real):
    gw = Q_PER_KV * HEAD_DIM
    q_tiles = seq // Q_TILE
    meta0 = n_real // N_META
    q_spec = pl.BlockSpec((Q_TILE, gw), lambda b, g, i: (b * q_tiles + i, g))
    qm_spec = pl.BlockSpec((N_META, gw), lambda b, g, i: (meta0 + b, g))
    kv_spec = pl.BlockSpec((seq, HEAD_DIM), lambda b, g, i: (b, g))
    kvm_spec = pl.BlockSpec((N_META, HEAD_DIM), lambda b, g, i: (meta0 + b, g))
    om_spec = pl.BlockSpec((N_META, gw), lambda b, g, i: (b, g))
    zeros = jnp.zeros((ROW_TILE, N_HEADS * HEAD_DIM), BF16)
    return pl.pallas_call(
        _attn_kernel,
        grid=(nseq, N_KV_HEADS, q_tiles),
        in_specs=[q_spec, qm_spec, kv_spec, kvm_spec, kv_spec, kvm_spec, pl.BlockSpec(memory_space=pl.ANY)],
        out_specs=[q_spec, om_spec],
        out_shape=[jax.ShapeDtypeStruct((n_real, N_HEADS * HEAD_DIM), BF16),
                   jax.ShapeDtypeStruct(zeros.shape, BF16)],
        scratch_shapes=[pltpu.VMEM((HEAD_DIM, seq), BF16), pltpu.VMEM((HEAD_DIM, LANES), BF16)],
        input_output_aliases={6: 1},
        compiler_params=_params("parallel", "parallel", "arbitrary"),
        name="attention",
    )(q, q, k, k, v, v, zeros)


def _oproj_kernel(*refs, counts):
    n = len(counts)
    o_ref, om_ref, w_ref, out_ref = refs[n:]
    is_meta_tile = pl.program_id(0) == pl.num_programs(0) - 1
    o = jnp.where(is_meta_tile, om_ref[...], o_ref[...])
    out_ref[...] = _read_row_tile(refs[:n], counts) + jnp.dot(o, w_ref[...], preferred_element_type=F32)


def _oproj_call(parts, o_real, o_meta, w_o, layer):
    d = w_o.shape[2]
    counts = tuple(n for _, n in parts)
    n_tiles = sum(counts)
    n_real_tiles = o_real.shape[0] // ROW_TILE
    o_spec = pl.BlockSpec((ROW_TILE, o_real.shape[1]), lambda i: (jnp.minimum(i, n_real_tiles - 1), 0))
    om_spec = pl.BlockSpec((ROW_TILE, o_real.shape[1]), lambda i: (0, 0))
    return pl.pallas_call(
        functools.partial(_oproj_kernel, counts=counts),
        grid=(n_tiles,),
        in_specs=_row_part_specs(parts, d) + [o_spec, om_spec, _layer_resident(w_o, layer)],
        out_specs=pl.BlockSpec((ROW_TILE, d), lambda i: (i, 0)),
        out_shape=jax.ShapeDtypeStruct((n_tiles * ROW_TILE, d), F32),
        compiler_params=_params("parallel"),
        name="attn_out_proj",
    )(*[a for a, _ in parts], o_real, o_meta, w_o)


def _mlp_kernel(h_ref, gain_ref, wu_ref, wd_ref, fgain_ref, out_ref, xn_ref, *, final_norm):
    k = pl.program_id(1)

    @pl.when(k == 0)
    def _():
        x = h_ref[...]
        xn_ref[...] = _rmsnorm(x, gain_ref[...]).astype(BF16)
        out_ref[...] = x

    u = jnp.dot(xn_ref[...], wu_ref[...], preferred_element_type=F32)
    a = jnp.square(jnp.maximum(u, 0.0)).astype(BF16)
    out_ref[...] += jnp.dot(a, wd_ref[...], preferred_element_type=F32)

    if final_norm:
        @pl.when(k == pl.num_programs(1) - 1)
        def _():
            out_ref[...] = _rmsnorm(out_ref[...], fgain_ref[...])


def _mlp_call(h, gain, w_up, w_down, layer, final_gain, first_row, n_rows, final_norm, row_tile, ff_tile):
    _, d, dff = w_up.shape
    first_tile, n_tiles = first_row // row_tile, n_rows // row_tile
    return pl.pallas_call(
        functools.partial(_mlp_kernel, final_norm=final_norm),
        grid=(n_tiles, dff // ff_tile),
        in_specs=[pl.BlockSpec((row_tile, d), lambda i, k: (first_tile + i, 0)), _vec_spec(d),
                  pl.BlockSpec((None, d, ff_tile), lambda i, k: (layer, 0, k)),
                  pl.BlockSpec((None, ff_tile, d), lambda i, k: (layer, k, 0)), _vec_spec(d)],
        out_specs=pl.BlockSpec((row_tile, d), lambda i, k: (i, 0)),
        out_shape=jax.ShapeDtypeStruct((n_rows, d), F32),
        scratch_shapes=[pltpu.VMEM((row_tile, d), BF16)],
        compiler_params=_params("parallel", "arbitrary"),
        name="mlp_final" if final_norm else "mlp",
    )(h, gain, w_up, w_down, final_gain)


def _band_matrices(n, kp):
    r = lax.broadcasted_iota(jnp.int32, (n, kp), 0)
    c = lax.broadcasted_iota(jnp.int32, (n, kp), 1)
    off = c - r - HALO
    return [jnp.where((off >= -(w // 2)) & (off < w // 2), 1.0, 0.0).astype(BF16) for w in POOL_WINDOWS]


def _extended_rows(prev, cur, nxt, kp):
    pad = jnp.zeros((kp - cur.shape[0] - 2 * HALO, cur.shape[1]), F32)
    return jnp.concatenate([prev, cur, nxt, pad], axis=0).astype(BF16)


def _pool_rows(x, xn, xe_ref, band_ref, first_pos, seq_len, w_ref, scale_ref):
    n = x.shape[0]
    gdim = w_ref.shape[1]
    pos = first_pos + lax.broadcasted_iota(jnp.int32, (n, LANES), 0)
    outs = []
    for g, w in enumerate(POOL_WINDOWS):
        cols = slice(g * gdim, (g + 1) * gdim)
        window_sum = jnp.dot(band_ref[g], xe_ref[:, cols], preferred_element_type=F32)
        cnt = (jnp.minimum(pos + w // 2, seq_len) - jnp.maximum(pos - w // 2, 0)).astype(F32)
        inv = jnp.tile(1.0 / cnt, (1, gdim // LANES))
        mixed = (window_sum * inv - xn[:, cols]).astype(BF16)
        y = jnp.dot(mixed, w_ref[g], preferred_element_type=F32)
        outs.append(x[:, cols] + y * scale_ref[:, cols])
    return jnp.concatenate(outs, axis=-1)


def _pool_kernel(*refs, seq, n_real_tiles, with_tail):
    cur_ref, prev_ref, next_ref, gain_ref, w_ref, scale_ref = refs[:6]
    out_ref, xe_ref, band_ref = refs[-3:]
    i = pl.program_id(0)
    tiles = seq // ROW_TILE
    j = i % tiles

    @pl.when(i == 0)
    def _():
        for g, band in enumerate(_band_matrices(ROW_TILE, xe_ref.shape[0])):
            band_ref[g] = band

    @pl.when(i < n_real_tiles)
    def _():
        gain = gain_ref[...]
        x = cur_ref[...]
        xn = _rmsnorm(x, gain)
        nxt = jnp.where(j == tiles - 1, 0.0, _rmsnorm(next_ref[...], gain))
        xe_ref[...] = _extended_rows(_rmsnorm(prev_ref[...], gain), xn, nxt, xe_ref.shape[0])
        out_ref[...] = _pool_rows(x, xn, xe_ref, band_ref, N_META + j * ROW_TILE, N_META + seq, w_ref, scale_ref)

    if with_tail:
        @pl.when(i == n_real_tiles)
        def _():
            out_ref[...] = refs[6][...]


def _pool_meta_kernel(cur_ref, next_ref, gain_ref, w_ref, scale_ref, out_ref, xe_ref, band_ref, *, nseq, seq):
    b = pl.program_id(0)

    @pl.when(b == 0)
    def _():
        for g, band in enumerate(_band_matrices(N_META, xe_ref.shape[0])):
            band_ref[g] = band

    @pl.when(b < nseq)
    def _():
        gain = gain_ref[...]
        x = cur_ref[...]
        xn = _rmsnorm(x, gain)
        before = jnp.zeros((HALO, x.shape[1]), F32)
        xe_ref[...] = _extended_rows(before, xn, _rmsnorm(next_ref[...], gain), xe_ref.shape[0])
        out_ref[...] = _pool_rows(x, xn, xe_ref, band_ref, 0, N_META + seq, w_ref, scale_ref)

    @pl.when(b >= nseq)
    def _():
        out_ref[...] = jnp.zeros(out_ref.shape, F32)


def _pool_call(h, gain, w_pool, layer, scale, nseq, seq, n_real, with_meta):
    d = h.shape[1]
    tiles = seq // ROW_TILE
    n_real_tiles = n_real // ROW_TILE
    blk = ROW_TILE // HALO
    meta_blk0 = n_real // HALO
    w_spec = _layer_resident(w_pool, layer)
    in_specs, operands = [], []
    if with_meta:
        meta0 = n_real // N_META
        seq_of = lambda b: jnp.minimum(b, nseq - 1)
        tail = pl.pallas_call(
            functools.partial(_pool_meta_kernel, nseq=nseq, seq=seq),
            grid=(ROW_TILE // N_META,),
            in_specs=[pl.BlockSpec((N_META, d), lambda b: (meta0 + seq_of(b), 0)),
                      pl.BlockSpec((HALO, d), lambda b: (seq_of(b) * tiles * blk, 0)),
                      _vec_spec(d), w_spec, _vec_spec(d)],
            out_specs=pl.BlockSpec((N_META, d), lambda b: (b, 0)),
            out_shape=jax.ShapeDtypeStruct((ROW_TILE, d), F32),
            scratch_shapes=[pltpu.VMEM((LANES, d), BF16), pltpu.VMEM((len(POOL_WINDOWS), N_META, LANES), BF16)],
            compiler_params=_params("arbitrary"),
            name="pool_mixer_meta",
        )(h, h, gain, w_pool, scale)
        in_specs, operands = [pl.BlockSpec((ROW_TILE, d), lambda i: (0, 0))], [tail]

    real = lambda i: jnp.minimum(i, n_real_tiles - 1)
    cur_spec = pl.BlockSpec((ROW_TILE, d), lambda i: (real(i), 0))
    prev_spec = pl.BlockSpec((HALO, d), lambda i: (
        jnp.where(real(i) % tiles == 0, meta_blk0 + 2 * (real(i) // tiles) + 1, real(i) * blk - 1), 0))
    next_spec = pl.BlockSpec((HALO, d), lambda i: (jnp.where(real(i) % tiles == tiles - 1, 0, (real(i) + 1) * blk), 0))
    n_out_tiles = n_real_tiles + (1 if with_meta else 0)
    ext_rows = -(-(ROW_TILE + 2 * HALO) // LANES) * LANES
    return pl.pallas_call(
        functools.partial(_pool_kernel, seq=seq, n_real_tiles=n_real_tiles, with_tail=with_meta),
        grid=(n_out_tiles,),
        in_specs=[cur_spec, prev_spec, next_spec, _vec_spec(d), w_spec, _vec_spec(d)] + in_specs,
        out_specs=pl.BlockSpec((ROW_TILE, d), lambda i: (i, 0)),
        out_shape=jax.ShapeDtypeStruct((n_out_tiles * ROW_TILE, d), F32),
        scratch_shapes=[pltpu.VMEM((ext_rows, d), BF16), pltpu.VMEM((len(POOL_WINDOWS), ROW_TILE, ext_rows), BF16)],
        compiler_params=_params("arbitrary"),
        name="pool_mixer",
    )(h, h, h, gain, w_pool, scale, *operands)


def kernel(x_prompt, x_sample, meta_tokens, attn_norm, w_qkv, q_norm, k_norm, w_o, pool_norm, w_pool, pool_scale,
           mlp_norm, w_up, w_down, final_norm):
    b1, seq, d = x_prompt.shape
    b2 = x_sample.shape[0]
    assert x_sample.shape[1:] == (seq, d) and seq % ROW_TILE == 0 and seq % GRID_W == 0
    nseq = b1 + b2
    n_real = nseq * seq
    assert nseq * N_META <= ROW_TILE
    depth = mlp_norm.shape[0]
    t1, t2 = b1 * seq // ROW_TILE, b2 * seq // ROW_TILE

    tail = jnp.concatenate([jnp.tile(meta_tokens.astype(F32), (nseq, 1)),
                            jnp.zeros((ROW_TILE - nseq * N_META, d), F32)], axis=0)
    parts = [(x_prompt.reshape(b1 * seq, d), t1), (x_sample.reshape(b2 * seq, d), t2), (tail, 1)]

    nqk = (N_HEADS + N_KV_HEADS) * HEAD_DIM
    tables = _rope_tables(seq)
    w_qkv = jnp.concatenate([_split_pairs(w_qkv[..., :nqk], N_HEADS + N_KV_HEADS), w_qkv[..., nqk:]], axis=-1)
    w_qkv, w_o, w_pool, w_up, w_down = (w.astype(BF16) for w in (w_qkv, w_o, w_pool, w_up, w_down))
    row2 = lambda a: a.reshape(1, -1).astype(F32)
    q_gain = _split_pairs(q_norm.astype(F32), 1) * (HEAD_DIM ** -0.5 * math.log2(math.e))
    k_gain = _split_pairs(k_norm.astype(F32), 1)

    outs = None
    h = None
    for i in range(depth):
        j = i // 2
        last = i == depth - 1
        if i % 2 == 0:
            q, k, v = _qkv_call(parts, row2(attn_norm[j]), w_qkv, j, row2(q_gain[j]), row2(k_gain[j]), tables, seq, n_real)
            o_real, o_meta = _attn_call(q, k, v, nseq, seq, n_real)
            h = _oproj_call(parts, o_real, o_meta, w_o, j)
        else:
            h = _pool_call(h, row2(pool_norm[j]), w_pool, j, row2(pool_scale[j]), nseq, seq, n_real, with_meta=not last)
        if not last:
            h = _mlp_call(h, row2(mlp_norm[i]), w_up, w_down, i, row2(final_norm), 0, h.shape[0], False,
                          ROW_TILE, FF_TILE)
            parts = [(h, h.shape[0] // ROW_TILE)]
        else:
            mlp = functools.partial(_mlp_call, h, row2(mlp_norm[i]), w_up, w_down, i, row2(final_norm))
            outs = (mlp(0, b1 * seq, True, 2 * ROW_TILE, FF_TILE // 2).reshape(b1, seq, d),
                    mlp(b1 * seq, b2 * seq, True, 2 * ROW_TILE, FF_TILE // 2).reshape(b2, seq, d))
    return outs
```

```python
import functools
import math

import numpy as np
import jax
import jax.numpy as jnp
from jax import lax
from jax.experimental import pallas as pl
from jax.experimental.pallas import tpu as pltpu

N_HEADS = 16
N_KV_HEADS = 4
Q_PER_KV = N_HEADS // N_KV_HEADS
HEAD_DIM = 128
ROPE_THETA = 10000.0
GRID_W = 64
N_META = 16
POOL_WINDOWS = (2, 4, 8, 16)
EPS = 1e-6

LANES = 128
F32_SUBLANES = 8
SUM_ROWS = 16
HALO = max(POOL_WINDOWS) // 2
ROW_TILE = 512
FF_TILE = 1024
Q_TILE = 512
MXU_COLS = 256
VMEM_LIMIT_BYTES = 56 * 1024 * 1024

assert HALO == F32_SUBLANES and N_META == 2 * HALO and HEAD_DIM == LANES

BF16 = jnp.bfloat16
F32 = jnp.float32


def _params(*semantics):
    return pltpu.CompilerParams(dimension_semantics=semantics, vmem_limit_bytes=VMEM_LIMIT_BYTES)


def _rmsnorm(x, gain):
    ms = jnp.mean(x * x, axis=-1, keepdims=True)
    return x * lax.rsqrt(ms + EPS) * gain


def _layer_resident(w, layer):
    zeros = (0,) * (w.ndim - 1)
    return pl.BlockSpec((None,) + w.shape[1:], lambda *_: (layer,) + zeros, pipeline_mode=pl.Buffered(1))


def _vec_spec(width):
    return pl.BlockSpec((1, width), lambda *_: (0, 0))


def _row_part_specs(parts, d):
    specs, start = [], 0
    for _, n in parts:
        index_map = functools.partial(lambda i, *_, start, n: (jnp.clip(i - start, 0, n - 1), 0), start=start, n=n)
        mode = dict(pipeline_mode=pl.Buffered(1)) if n == 1 and len(parts) > 1 else {}
        specs.append(pl.BlockSpec((ROW_TILE, d), index_map, **mode))
        start += n
    return specs


def _read_row_tile(refs, counts):
    i = pl.program_id(0)
    x = refs[0][...]
    start = 0
    for ref, n in zip(refs[1:], counts[:-1]):
        start += n
        x = jnp.where(i >= start, ref[...], x)
    return x


def _split_pairs(w, n_heads):
    lead = w.shape[:-1]
    return w.reshape(*lead, n_heads, HEAD_DIM // 2, 2).swapaxes(-1, -2).reshape(*lead, n_heads * HEAD_DIM)


def _rope_tables(seq):
    t = np.arange((seq // GRID_W) * GRID_W)
    r = (t // GRID_W).astype(np.float32)
    c = (t % GRID_W).astype(np.float32)
    axis_dim = HEAD_DIM // 2
    inv_freq = (ROPE_THETA ** (-np.arange(0, axis_dim, 2, dtype=np.float32) / axis_dim)).astype(np.float32)
    ang = np.concatenate([r[:, None] * inv_freq[None], c[:, None] * inv_freq[None]], axis=-1)
    cos, sin = np.cos(ang).astype(np.float32), np.sin(ang).astype(np.float32)
    cos = np.concatenate([cos, cos], axis=-1)
    sin = np.concatenate([-sin, sin], axis=-1)
    ident = np.ones((ROW_TILE, HEAD_DIM), np.float32)
    zero = np.zeros((ROW_TILE, HEAD_DIM), np.float32)
    return jnp.asarray(np.concatenate([cos, ident])), jnp.asarray(np.concatenate([sin, zero]))


def _qkv_kernel(*refs, counts):
    n = len(counts)
    gain_ref, w_ref, qg_ref, kg_ref, cos_ref, sin_ref, q_ref, k_ref, v_ref, y_ref = refs[n:]
    i = pl.program_id(0)
    n_tiles = pl.num_programs(0) - 1
    nq, nk = q_ref.shape[1], k_ref.shape[1]

    def finish(col):
        gain = qg_ref[...] if col < nq else kg_ref[...]
        yn = _rmsnorm(y_ref[:, col:col + HEAD_DIM], gain)
        out = (yn * cos_ref[...] + pltpu.roll(yn, HEAD_DIM // 2, 1) * sin_ref[...]).astype(BF16)
        if col < nq:
            q_ref[:, col:col + HEAD_DIM] = out
        else:
            k_ref[:, col - nq:col - nq + HEAD_DIM] = out

    def step(project, finish_previous):
        if project:
            xn = _rmsnorm(_read_row_tile(refs[:n], counts), gain_ref[...]).astype(BF16)
        for c0 in range(0, nq + 2 * nk, MXU_COLS):
            head_cols = range(c0, c0 + MXU_COLS, HEAD_DIM)
            if finish_previous:
                for col in head_cols:
                    if col < nq + nk:
                        finish(col)
            if project:
                y = jnp.dot(xn, w_ref[:, c0:c0 + MXU_COLS], preferred_element_type=F32)
                for col in head_cols:
                    yh = y[:, col - c0:col - c0 + HEAD_DIM]
                    if col < nq + nk:
                        y_ref[:, col:col + HEAD_DIM] = yh
                    else:
                        v_ref[:, col - nq - nk:col - nq - nk + HEAD_DIM] = yh.astype(BF16)

    pl.when(i == 0)(functools.partial(step, True, False))
    pl.when((i > 0) & (i < n_tiles))(functools.partial(step, True, True))
    pl.when(i == n_tiles)(functools.partial(step, False, True))


def _qkv_call(parts, gain, w_qkv, layer, q_gain, k_gain, tables, seq, n_real):
    d = w_qkv.shape[1]
    counts = tuple(n for _, n in parts)
    n_tiles = sum(counts)
    ntok = n_tiles * ROW_TILE
    nq, nk = N_HEADS * HEAD_DIM, N_KV_HEADS * HEAD_DIM
    tiles_per_seq, n_real_tiles = seq // ROW_TILE, n_real // ROW_TILE
    finished = lambda i: jnp.maximum(i - 1, 0)
    row = lambda width, tile: pl.BlockSpec((ROW_TILE, width), lambda i: (tile(i), 0))
    table = pl.BlockSpec((ROW_TILE, HEAD_DIM), lambda i: (
        jnp.where(finished(i) < n_real_tiles, finished(i) % tiles_per_seq, tiles_per_seq), 0))
    return pl.pallas_call(
        functools.partial(_qkv_kernel, counts=counts),
        grid=(n_tiles + 1,),
        in_specs=_row_part_specs(parts, d) + [_vec_spec(d), _layer_resident(w_qkv, layer), _vec_spec(HEAD_DIM),
                                              _vec_spec(HEAD_DIM), table, table],
        out_specs=[row(nq, finished), row(nk, finished), row(nk, lambda i: jnp.minimum(i, n_tiles - 1))],
        out_shape=[jax.ShapeDtypeStruct((ntok, nq), BF16),
                   jax.ShapeDtypeStruct((ntok, nk), BF16),
                   jax.ShapeDtypeStruct((ntok, nk), BF16)],
        scratch_shapes=[pltpu.VMEM((ROW_TILE, nq + nk), F32)],
        compiler_params=_params("arbitrary"),
        name="qkv_proj",
    )(*[a for a, _ in parts], gain, w_qkv, q_gain, k_gain, *tables)


def _attn_kernel(q_ref, qm_ref, k_ref, km_ref, v_ref, vm_ref, zeros_ref, o_ref, om_ref, vt_ref, vtpad_ref):
    del zeros_ref
    first_q_tile = pl.program_id(2) == 0

    @pl.when(first_q_tile)
    def _():
        pad = jnp.zeros((LANES - N_META, HEAD_DIM), BF16)
        vt_ref[0:HEAD_DIM, :] = v_ref[...].astype(F32).T.astype(BF16)
        vt_ref[HEAD_DIM:, :] = jnp.ones((SUM_ROWS, v_ref.shape[0]), BF16)
        vtpad_ref[0:HEAD_DIM, :] = jnp.concatenate([vm_ref[...], pad], axis=0).astype(F32).T.astype(BF16)
        vtpad_ref[HEAD_DIM:, :] = jnp.ones((SUM_ROWS, LANES), BF16)

    def scores(qh):
        contract_last = (((1,), (1,)), ((), ()))
        st = lax.dot_general(k_ref[...], qh, contract_last, preferred_element_type=F32)
        stm = lax.dot_general(km_ref[...], qh, contract_last, preferred_element_type=F32)
        return st, stm

    def softmax_pv(st, stm):
        m = jnp.maximum(jnp.max(st, axis=0, keepdims=True), jnp.max(stm, axis=0, keepdims=True))
        p, pm = jnp.exp2((st - m).astype(BF16)), jnp.exp2((stm - m).astype(BF16))
        pm = jnp.concatenate([pm, jnp.zeros((LANES - N_META, pm.shape[1]), BF16)], axis=0)
        ot = (jnp.dot(vt_ref[...], p, preferred_element_type=F32)
              + jnp.dot(vtpad_ref[...], pm, preferred_element_type=F32))
        return (ot[0:HEAD_DIM] * (1.0 / ot[HEAD_DIM:HEAD_DIM + 1])).T

    def attend(qh):
        return softmax_pv(*scores(qh))

    head_cols = [slice(hq * HEAD_DIM, (hq + 1) * HEAD_DIM) for hq in range(Q_PER_KV)]
    head_scores = [scores(q_ref[:, cols]) for cols in head_cols]
    for cols, s in zip(head_cols, head_scores):
        o_ref[:, cols] = softmax_pv(*s).astype(BF16)

    @pl.when(first_q_tile)
    def _():
        qm = [qm_ref[:, hq * HEAD_DIM:(hq + 1) * HEAD_DIM] for hq in range(Q_PER_KV)]
        qm.append(jnp.zeros((LANES - Q_PER_KV * N_META, HEAD_DIM), BF16))
        om = attend(jnp.concatenate(qm, axis=0))
        for hq in range(Q_PER_KV):
            om_ref[:, hq * HEAD_DIM:(hq + 1) * HEAD_DIM] = om[hq * N_META:(hq + 1) * N_META].astype(BF16)


def _attn_call(q, k, v, nseq, seq, n_real):
    gw = Q_PER_KV * HEAD_DIM
    q_tiles = seq // Q_TILE
    meta0 = n_real // N_META
    q_spec = pl.BlockSpec((Q_TILE, gw), lambda b, g, i: (b * q_tiles + i, g))
    qm_spec = pl.BlockSpec((N_META, gw), lambda b, g, i: (meta0 + b, g))
    kv_spec = pl.BlockSpec((seq, HEAD_DIM), lambda b, g, i: (b, g))
    kvm_spec = pl.BlockSpec((N_META, HEAD_DIM), lambda b, g, i: (meta0 + b, g))
    om_spec = pl.BlockSpec((N_META, gw), lambda b, g, i: (b, g))
    zeros = jnp.zeros((ROW_TILE, N_HEADS * HEAD_DIM), BF16)
    return pl.pallas_call(
        _attn_kernel,
        grid=(nseq, N_KV_HEADS, q_tiles),
        in_specs=[q_spec, qm_spec, kv_spec, kvm_spec, kv_spec, kvm_spec, pl.BlockSpec(memory_space=pl.ANY)],
        out_specs=[q_spec, om_spec],
        out_shape=[jax.ShapeDtypeStruct((n_real, N_HEADS * HEAD_DIM), BF16),
                   jax.ShapeDtypeStruct(zeros.shape, BF16)],
        scratch_shapes=[pltpu.VMEM((HEAD_DIM + SUM_ROWS, seq), BF16), pltpu.VMEM((HEAD_DIM + SUM_ROWS, LANES), BF16)],
        input_output_aliases={6: 1},
        compiler_params=_params("parallel", "parallel", "arbitrary"),
        name="attention",
    )(q, q, k, k, v, v, zeros)


def _oproj_kernel(*refs, counts):
    n = len(counts)
    o_ref, om_ref, w_ref, out_ref = refs[n:]
    is_meta_tile = pl.program_id(0) == pl.num_programs(0) - 1
    o = jnp.where(is_meta_tile, om_ref[...], o_ref[...])
    out_ref[...] = _read_row_tile(refs[:n], counts) + jnp.dot(o, w_ref[...], preferred_element_type=F32)


def _oproj_call(parts, o_real, o_meta, w_o, layer):
    d = w_o.shape[2]
    counts = tuple(n for _, n in parts)
    n_tiles = sum(counts)
    n_real_tiles = o_real.shape[0] // ROW_TILE
    o_spec = pl.BlockSpec((ROW_TILE, o_real.shape[1]), lambda i: (jnp.minimum(i, n_real_tiles - 1), 0))
    om_spec = pl.BlockSpec((ROW_TILE, o_real.shape[1]), lambda i: (0, 0))
    return pl.pallas_call(
        functools.partial(_oproj_kernel, counts=counts),
        grid=(n_tiles,),
        in_specs=_row_part_specs(parts, d) + [o_spec, om_spec, _layer_resident(w_o, layer)],
        out_specs=pl.BlockSpec((ROW_TILE, d), lambda i: (i, 0)),
        out_shape=jax.ShapeDtypeStruct((n_tiles * ROW_TILE, d), F32),
        compiler_params=_params("parallel"),
        name="attn_out_proj",
    )(*[a for a, _ in parts], o_real, o_meta, w_o)


def _mlp_kernel(h_ref, gain_ref, wu_ref, wd_ref, fgain_ref, out_ref, xn_ref, *, final_norm):
    k = pl.program_id(1)

    @pl.when(k == 0)
    def _():
        x = h_ref[...]
        xn_ref[...] = _rmsnorm(x, gain_ref[...]).astype(BF16)
        out_ref[...] = x

    u = jnp.dot(xn_ref[...], wu_ref[...], preferred_element_type=F32)
    a = jnp.square(jnp.maximum(u, 0.0)).astype(BF16)
    out_ref[...] += jnp.dot(a, wd_ref[...], preferred_element_type=F32)

    if final_norm:
        @pl.when(k == pl.num_programs(1) - 1)
        def _():
            out_ref[...] = _rmsnorm(out_ref[...], fgain_ref[...])


def _mlp_call(h, gain, w_up, w_down, layer, final_gain, first_tile, n_tiles, final_norm):
    _, d, dff = w_up.shape
    return pl.pallas_call(
        functools.partial(_mlp_kernel, final_norm=final_norm),
        grid=(n_tiles, dff // FF_TILE),
        in_specs=[pl.BlockSpec((ROW_TILE, d), lambda i, k: (first_tile + i, 0)), _vec_spec(d),
                  pl.BlockSpec((None, d, FF_TILE), lambda i, k: (layer, 0, k)),
                  pl.BlockSpec((None, FF_TILE, d), lambda i, k: (layer, k, 0)), _vec_spec(d)],
        out_specs=pl.BlockSpec((ROW_TILE, d), lambda i, k: (i, 0)),
        out_shape=jax.ShapeDtypeStruct((n_tiles * ROW_TILE, d), F32),
        scratch_shapes=[pltpu.VMEM((ROW_TILE, d), BF16)],
        compiler_params=_params("parallel", "arbitrary"),
        name="mlp_final" if final_norm else "mlp",
    )(h, gain, w_up, w_down, final_gain)


def _band_matrices(n, kp):
    r = lax.broadcasted_iota(jnp.int32, (n, kp), 0)
    c = lax.broadcasted_iota(jnp.int32, (n, kp), 1)
    off = c - r - HALO
    return [jnp.where((off >= -(w // 2)) & (off < w // 2), 1.0, 0.0).astype(BF16) for w in POOL_WINDOWS]


def _extended_rows(prev, cur, nxt, kp):
    pad = jnp.zeros((kp - cur.shape[0] - 2 * HALO, cur.shape[1]), F32)
    return jnp.concatenate([prev, cur, nxt, pad], axis=0).astype(BF16)


def _pool_rows(x, xn, xe_ref, band_ref, first_pos, seq_len, w_ref, scale_ref):
    n = x.shape[0]
    gdim = w_ref.shape[1]
    pos = first_pos + lax.broadcasted_iota(jnp.int32, (n, LANES), 0)
    outs = []
    for g, w in enumerate(POOL_WINDOWS):
        cols = slice(g * gdim, (g + 1) * gdim)
        window_sum = jnp.dot(band_ref[g], xe_ref[:, cols], preferred_element_type=F32)
        cnt = (jnp.minimum(pos + w // 2, seq_len) - jnp.maximum(pos - w // 2, 0)).astype(F32)
        inv = jnp.tile(1.0 / cnt, (1, gdim // LANES))
        mixed = (window_sum * inv - xn[:, cols]).astype(BF16)
        y = jnp.dot(mixed, w_ref[g], preferred_element_type=F32)
        outs.append(x[:, cols] + y * scale_ref[:, cols])
    return jnp.concatenate(outs, axis=-1)


def _pool_kernel(*refs, seq, n_real_tiles, with_tail):
    cur_ref, prev_ref, next_ref, gain_ref, w_ref, scale_ref = refs[:6]
    out_ref, xe_ref, band_ref = refs[-3:]
    i = pl.program_id(0)
    tiles = seq // ROW_TILE
    j = i % tiles

    @pl.when(i == 0)
    def _():
        for g, band in enumerate(_band_matrices(ROW_TILE, xe_ref.shape[0])):
            band_ref[g] = band

    @pl.when(i < n_real_tiles)
    def _():
        gain = gain_ref[...]
        x = cur_ref[...]
        xn = _rmsnorm(x, gain)
        nxt = jnp.where(j == tiles - 1, 0.0, _rmsnorm(next_ref[...], gain))
        xe_ref[...] = _extended_rows(_rmsnorm(prev_ref[...], gain), xn, nxt, xe_ref.shape[0])
        out_ref[...] = _pool_rows(x, xn, xe_ref, band_ref, N_META + j * ROW_TILE, N_META + seq, w_ref, scale_ref)

    if with_tail:
        @pl.when(i == n_real_tiles)
        def _():
            out_ref[...] = refs[6][...]


def _pool_meta_kernel(cur_ref, next_ref, gain_ref, w_ref, scale_ref, out_ref, xe_ref, band_ref, *, nseq, seq):
    b = pl.program_id(0)

    @pl.when(b == 0)
    def _():
        for g, band in enumerate(_band_matrices(N_META, xe_ref.shape[0])):
            band_ref[g] = band

    @pl.when(b < nseq)
    def _():
        gain = gain_ref[...]
        x = cur_ref[...]
        xn = _rmsnorm(x, gain)
        before = jnp.zeros((HALO, x.shape[1]), F32)
        xe_ref[...] = _extended_rows(before, xn, _rmsnorm(next_ref[...], gain), xe_ref.shape[0])
        out_ref[...] = _pool_rows(x, xn, xe_ref, band_ref, 0, N_META + seq, w_ref, scale_ref)

    @pl.when(b >= nseq)
    def _():
        out_ref[...] = jnp.zeros(out_ref.shape, F32)


def _pool_call(h, gain, w_pool, layer, scale, nseq, seq, n_real, with_meta):
    d = h.shape[1]
    tiles = seq // ROW_TILE
    n_real_tiles = n_real // ROW_TILE
    blk = ROW_TILE // HALO
    meta_blk0 = n_real // HALO
    w_spec = _layer_resident(w_pool, layer)
    in_specs, operands = [], []
    if with_meta:
        meta0 = n_real // N_META
        seq_of = lambda b: jnp.minimum(b, nseq - 1)
        tail = pl.pallas_call(
            functools.partial(_pool_meta_kernel, nseq=nseq, seq=seq),
            grid=(ROW_TILE // N_META,),
            in_specs=[pl.BlockSpec((N_META, d), lambda b: (meta0 + seq_of(b), 0)),
                      pl.BlockSpec((HALO, d), lambda b: (seq_of(b) * tiles * blk, 0)),
                      _vec_spec(d), w_spec, _vec_spec(d)],
            out_specs=pl.BlockSpec((N_META, d), lambda b: (b, 0)),
            out_shape=jax.ShapeDtypeStruct((ROW_TILE, d), F32),
            scratch_shapes=[pltpu.VMEM((LANES, d), BF16), pltpu.VMEM((len(POOL_WINDOWS), N_META, LANES), BF16)],
            compiler_params=_params("arbitrary"),
            name="pool_mixer_meta",
        )(h, h, gain, w_pool, scale)
        in_specs, operands = [pl.BlockSpec((ROW_TILE, d), lambda i: (0, 0))], [tail]

    real = lambda i: jnp.minimum(i, n_real_tiles - 1)
    cur_spec = pl.BlockSpec((ROW_TILE, d), lambda i: (real(i), 0))
    prev_spec = pl.BlockSpec((HALO, d), lambda i: (
        jnp.where(real(i) % tiles == 0, meta_blk0 + 2 * (real(i) // tiles) + 1, real(i) * blk - 1), 0))
    next_spec = pl.BlockSpec((HALO, d), lambda i: (jnp.where(real(i) % tiles == tiles - 1, 0, (real(i) + 1) * blk), 0))
    n_out_tiles = n_real_tiles + (1 if with_meta else 0)
    ext_rows = -(-(ROW_TILE + 2 * HALO) // LANES) * LANES
    return pl.pallas_call(
        functools.partial(_pool_kernel, seq=seq, n_real_tiles=n_real_tiles, with_tail=with_meta),
        grid=(n_out_tiles,),
        in_specs=[cur_spec, prev_spec, next_spec, _vec_spec(d), w_spec, _vec_spec(d)] + in_specs,
        out_specs=pl.BlockSpec((ROW_TILE, d), lambda i: (i, 0)),
        out_shape=jax.ShapeDtypeStruct((n_out_tiles * ROW_TILE, d), F32),
        scratch_shapes=[pltpu.VMEM((ext_rows, d), BF16), pltpu.VMEM((len(POOL_WINDOWS), ROW_TILE, ext_rows), BF16)],
        compiler_params=_params("arbitrary"),
        name="pool_mixer",
    )(h, h, h, gain, w_pool, scale, *operands)


def kernel(x_prompt, x_sample, meta_tokens, attn_norm, w_qkv, q_norm, k_norm, w_o, pool_norm, w_pool, pool_scale,
           mlp_norm, w_up, w_down, final_norm):
    b1, seq, d = x_prompt.shape
    b2 = x_sample.shape[0]
    assert x_sample.shape[1:] == (seq, d) and seq % ROW_TILE == 0 and seq % GRID_W == 0
    nseq = b1 + b2
    n_real = nseq * seq
    assert nseq * N_META <= ROW_TILE
    depth = mlp_norm.shape[0]
    t1, t2 = b1 * seq // ROW_TILE, b2 * seq // ROW_TILE

    tail = jnp.concatenate([jnp.tile(meta_tokens.astype(F32), (nseq, 1)),
                            jnp.zeros((ROW_TILE - nseq * N_META, d), F32)], axis=0)
    parts = [(x_prompt.reshape(b1 * seq, d), t1), (x_sample.reshape(b2 * seq, d), t2), (tail, 1)]

    nqk = (N_HEADS + N_KV_HEADS) * HEAD_DIM
    tables = _rope_tables(seq)
    w_qkv = jnp.concatenate([_split_pairs(w_qkv[..., :nqk], N_HEADS + N_KV_HEADS), w_qkv[..., nqk:]], axis=-1)
    w_qkv, w_o, w_pool, w_up, w_down = (w.astype(BF16) for w in (w_qkv, w_o, w_pool, w_up, w_down))
    row2 = lambda a: a.reshape(1, -1).astype(F32)
    q_gain = _split_pairs(q_norm.astype(F32), 1) * (HEAD_DIM ** -0.5 * math.log2(math.e))
    k_gain = _split_pairs(k_norm.astype(F32), 1)

    outs = None
    h = None
    for i in range(depth):
        j = i // 2
        last = i == depth - 1
        if i % 2 == 0:
            q, k, v = _qkv_call(parts, row2(attn_norm[j]), w_qkv, j, row2(q_gain[j]), row2(k_gain[j]), tables, seq, n_real)
            o_real, o_meta = _attn_call(q, k, v, nseq, seq, n_real)
            h = _oproj_call(parts, o_real, o_meta, w_o, j)
        else:
            h = _pool_call(h, row2(pool_norm[j]), w_pool, j, row2(pool_scale[j]), nseq, seq, n_real, with_meta=not last)
        if not last:
            h = _mlp_call(h, row2(mlp_norm[i]), w_up, w_down, i, row2(final_norm), 0, h.shape[0] // ROW_TILE, False)
            parts = [(h, h.shape[0] // ROW_TILE)]
        else:
            mlp = functools.partial(_mlp_call, h, row2(mlp_norm[i]), w_up, w_down, i, row2(final_norm))
            outs = (mlp(0, t1, True).reshape(b1, seq, d), mlp(t1, t2, True).reshape(b2, seq, d))
    return outs
```

```python
import functools
import math

import numpy as np
import jax
import jax.numpy as jnp
from jax import lax
from jax.experimental import pallas as pl
from jax.experimental.pallas import tpu as pltpu

N_HEADS = 16
N_KV_HEADS = 4
Q_PER_KV = N_HEADS // N_KV_HEADS
HEAD_DIM = 128
ROPE_THETA = 10000.0
GRID_W = 64
N_META = 16
POOL_WINDOWS = (2, 4, 8, 16)
EPS = 1e-6

LANES = 128
F32_SUBLANES = 8
HALO = max(POOL_WINDOWS) // 2
ROW_TILE = 512
FF_TILE = 1024
Q_TILE = 2048
MXU_COLS = 256
VMEM_LIMIT_BYTES = 56 * 1024 * 1024

assert HALO == F32_SUBLANES and N_META == 2 * HALO and HEAD_DIM == LANES

BF16 = jnp.bfloat16
F32 = jnp.float32


def _params(*semantics):
    return pltpu.CompilerParams(dimension_semantics=semantics, vmem_limit_bytes=VMEM_LIMIT_BYTES)


def _rmsnorm(x, gain):
    ms = jnp.mean(x * x, axis=-1, keepdims=True)
    return x * lax.rsqrt(ms + EPS) * gain


def _layer_resident(w, layer):
    zeros = (0,) * (w.ndim - 1)
    return pl.BlockSpec((None,) + w.shape[1:], lambda *_: (layer,) + zeros, pipeline_mode=pl.Buffered(1))


def _vec_spec(width):
    return pl.BlockSpec((1, width), lambda *_: (0, 0))


def _row_part_specs(parts, d):
    specs, start = [], 0
    for _, n in parts:
        index_map = functools.partial(lambda i, *_, start, n: (jnp.clip(i - start, 0, n - 1), 0), start=start, n=n)
        mode = dict(pipeline_mode=pl.Buffered(1)) if n == 1 and len(parts) > 1 else {}
        specs.append(pl.BlockSpec((ROW_TILE, d), index_map, **mode))
        start += n
    return specs


def _read_row_tile(refs, counts):
    i = pl.program_id(0)
    x = refs[0][...]
    start = 0
    for ref, n in zip(refs[1:], counts[:-1]):
        start += n
        x = jnp.where(i >= start, ref[...], x)
    return x


def _split_pairs(w, n_heads):
    lead = w.shape[:-1]
    return w.reshape(*lead, n_heads, HEAD_DIM // 2, 2).swapaxes(-1, -2).reshape(*lead, n_heads * HEAD_DIM)


def _rope_tables(seq):
    t = np.arange((seq // GRID_W) * GRID_W)
    r = (t // GRID_W).astype(np.float32)
    c = (t % GRID_W).astype(np.float32)
    axis_dim = HEAD_DIM // 2
    inv_freq = (ROPE_THETA ** (-np.arange(0, axis_dim, 2, dtype=np.float32) / axis_dim)).astype(np.float32)
    ang = np.concatenate([r[:, None] * inv_freq[None], c[:, None] * inv_freq[None]], axis=-1)
    cos, sin = np.cos(ang).astype(np.float32), np.sin(ang).astype(np.float32)
    cos = np.concatenate([cos, cos], axis=-1)
    sin = np.concatenate([-sin, sin], axis=-1)
    ident = np.ones((ROW_TILE, HEAD_DIM), np.float32)
    zero = np.zeros((ROW_TILE, HEAD_DIM), np.float32)
    return jnp.asarray(np.concatenate([cos, ident])), jnp.asarray(np.concatenate([sin, zero]))


def _qkv_kernel(*refs, counts):
    n = len(counts)
    gain_ref, w_ref, qg_ref, kg_ref, cos_ref, sin_ref, q_ref, k_ref, v_ref, y_ref = refs[n:]
    i = pl.program_id(0)
    n_tiles = pl.num_programs(0) - 1
    nq, nk = q_ref.shape[1], k_ref.shape[1]

    def finish(col):
        gain = qg_ref[...] if col < nq else kg_ref[...]
        yn = _rmsnorm(y_ref[:, col:col + HEAD_DIM], gain)
        out = (yn * cos_ref[...] + pltpu.roll(yn, HEAD_DIM // 2, 1) * sin_ref[...]).astype(BF16)
        if col < nq:
            q_ref[:, col:col + HEAD_DIM] = out
        else:
            k_ref[:, col - nq:col - nq + HEAD_DIM] = out

    def step(project, finish_previous):
        if project:
            xn = _rmsnorm(_read_row_tile(refs[:n], counts), gain_ref[...]).astype(BF16)
        for c0 in range(0, nq + 2 * nk, MXU_COLS):
            head_cols = range(c0, c0 + MXU_COLS, HEAD_DIM)
            if finish_previous:
                for col in head_cols:
                    if col < nq + nk:
                        finish(col)
            if project:
                y = jnp.dot(xn, w_ref[:, c0:c0 + MXU_COLS], preferred_element_type=F32)
                for col in head_cols:
                    yh = y[:, col - c0:col - c0 + HEAD_DIM]
                    if col < nq + nk:
                        y_ref[:, col:col + HEAD_DIM] = yh
                    else:
                        v_ref[:, col - nq - nk:col - nq - nk + HEAD_DIM] = yh.astype(BF16)

    pl.when(i == 0)(functools.partial(step, True, False))
    pl.when((i > 0) & (i < n_tiles))(functools.partial(step, True, True))
    pl.when(i == n_tiles)(functools.partial(step, False, True))


def _qkv_call(parts, gain, w_qkv, layer, q_gain, k_gain, tables, seq, n_real):
    d = w_qkv.shape[1]
    counts = tuple(n for _, n in parts)
    n_tiles = sum(counts)
    ntok = n_tiles * ROW_TILE
    nq, nk = N_HEADS * HEAD_DIM, N_KV_HEADS * HEAD_DIM
    tiles_per_seq, n_real_tiles = seq // ROW_TILE, n_real // ROW_TILE
    finished = lambda i: jnp.maximum(i - 1, 0)
    row = lambda width, tile: pl.BlockSpec((ROW_TILE, width), lambda i: (tile(i), 0))
    table = pl.BlockSpec((ROW_TILE, HEAD_DIM), lambda i: (
        jnp.where(finished(i) < n_real_tiles, finished(i) % tiles_per_seq, tiles_per_seq), 0))
    return pl.pallas_call(
        functools.partial(_qkv_kernel, counts=counts),
        grid=(n_tiles + 1,),
        in_specs=_row_part_specs(parts, d) + [_vec_spec(d), _layer_resident(w_qkv, layer), _vec_spec(HEAD_DIM),
                                              _vec_spec(HEAD_DIM), table, table],
        out_specs=[row(nq, finished), row(nk, finished), row(nk, lambda i: jnp.minimum(i, n_tiles - 1))],
        out_shape=[jax.ShapeDtypeStruct((ntok, nq), BF16),
                   jax.ShapeDtypeStruct((ntok, nk), BF16),
                   jax.ShapeDtypeStruct((ntok, nk), BF16)],
        scratch_shapes=[pltpu.VMEM((ROW_TILE, nq + nk), F32)],
        compiler_params=_params("arbitrary"),
        name="qkv_proj",
    )(*[a for a, _ in parts], gain, w_qkv, q_gain, k_gain, *tables)


def _attn_kernel(q_ref, qm_ref, k_ref, km_ref, v_ref, vm_ref, zeros_ref, o_ref, om_ref, vt_ref, vtpad_ref):
    del zeros_ref
    first_q_tile = pl.program_id(2) == 0

    @pl.when(first_q_tile)
    def _():
        pad = jnp.zeros((LANES - N_META, HEAD_DIM), BF16)
        vt_ref[...] = v_ref[...].astype(F32).T.astype(BF16)
        vtpad_ref[...] = jnp.concatenate([vm_ref[...], pad], axis=0).astype(F32).T.astype(BF16)

    def scores(qh):
        contract_last = (((1,), (1,)), ((), ()))
        st = lax.dot_general(k_ref[...], qh, contract_last, preferred_element_type=F32)
        stm = lax.dot_general(km_ref[...], qh, contract_last, preferred_element_type=F32)
        return st, stm

    def softmax_pv(st, stm):
        m = jnp.maximum(jnp.max(st, axis=0, keepdims=True), jnp.max(stm, axis=0, keepdims=True))
        p, pm = jnp.exp2(st - m), jnp.exp2(stm - m)
        denom = jnp.sum(p, axis=0, keepdims=True) + jnp.sum(pm, axis=0, keepdims=True)
        pm = jnp.concatenate([pm.astype(BF16), jnp.zeros((LANES - N_META, pm.shape[1]), BF16)], axis=0)
        ot = (jnp.dot(vt_ref[...], p.astype(BF16), preferred_element_type=F32)
              + jnp.dot(vtpad_ref[...], pm, preferred_element_type=F32))
        return (ot * (1.0 / denom)).T

    def attend(qh):
        return softmax_pv(*scores(qh))

    for hq in range(Q_PER_KV):
        cols = slice(hq * HEAD_DIM, (hq + 1) * HEAD_DIM)
        o_ref[:, cols] = attend(q_ref[:, cols]).astype(BF16)

    @pl.when(first_q_tile)
    def _():
        qm = [qm_ref[:, hq * HEAD_DIM:(hq + 1) * HEAD_DIM] for hq in range(Q_PER_KV)]
        qm.append(jnp.zeros((LANES - Q_PER_KV * N_META, HEAD_DIM), BF16))
        om = attend(jnp.concatenate(qm, axis=0))
        for hq in range(Q_PER_KV):
            om_ref[:, hq * HEAD_DIM:(hq + 1) * HEAD_DIM] = om[hq * N_META:(hq + 1) * N_META].astype(BF16)


def _attn_call(q, k, v, nseq, seq, n_real):
    gw = Q_PER_KV * HEAD_DIM
    q_tiles = seq // Q_TILE
    meta0 = n_real // N_META
    q_spec = pl.BlockSpec((Q_TILE, gw), lambda b, g, i: (b * q_tiles + i, g))
    qm_spec = pl.BlockSpec((N_META, gw), lambda b, g, i: (meta0 + b, g))
    kv_spec = pl.BlockSpec((seq, HEAD_DIM), lambda b, g, i: (b, g))
    kvm_spec = pl.BlockSpec((N_META, HEAD_DIM), lambda b, g, i: (meta0 + b, g))
    om_spec = pl.BlockSpec((N_META, gw), lambda b, g, i: (b, g))
    zeros = jnp.zeros((ROW_TILE, N_HEADS * HEAD_DIM), BF16)
    return pl.pallas_call(
        _attn_kernel,
        grid=(nseq, N_KV_HEADS, q_tiles),
        in_specs=[q_spec, qm_spec, kv_spec, kvm_spec, kv_spec, kvm_spec, pl.BlockSpec(memory_space=pl.ANY)],
        out_specs=[q_spec, om_spec],
        out_shape=[jax.ShapeDtypeStruct((n_real, N_HEADS * HEAD_DIM), BF16),
                   jax.ShapeDtypeStruct(zeros.shape, BF16)],
        scratch_shapes=[pltpu.VMEM((HEAD_DIM, seq), BF16), pltpu.VMEM((HEAD_DIM, LANES), BF16)],
        input_output_aliases={6: 1},
        compiler_params=_params("parallel", "parallel", "arbitrary"),
        name="attention",
    )(q, q, k, k, v, v, zeros)


def _oproj_kernel(*refs, counts):
    n = len(counts)
    o_ref, om_ref, w_ref, out_ref = refs[n:]
    is_meta_tile = pl.program_id(0) == pl.num_programs(0) - 1
    o = jnp.where(is_meta_tile, om_ref[...], o_ref[...])
    out_ref[...] = _read_row_tile(refs[:n], counts) + jnp.dot(o, w_ref[...], preferred_element_type=F32)


def _oproj_call(parts, o_real, o_meta, w_o, layer):
    d = w_o.shape[2]
    counts = tuple(n for _, n in parts)
    n_tiles = sum(counts)
    n_real_tiles = o_real.shape[0] // ROW_TILE
    o_spec = pl.BlockSpec((ROW_TILE, o_real.shape[1]), lambda i: (jnp.minimum(i, n_real_tiles - 1), 0))
    om_spec = pl.BlockSpec((ROW_TILE, o_real.shape[1]), lambda i: (0, 0))
    return pl.pallas_call(
        functools.partial(_oproj_kernel, counts=counts),
        grid=(n_tiles,),
        in_specs=_row_part_specs(parts, d) + [o_spec, om_spec, _layer_resident(w_o, layer)],
        out_specs=pl.BlockSpec((ROW_TILE, d), lambda i: (i, 0)),
        out_shape=jax.ShapeDtypeStruct((n_tiles * ROW_TILE, d), F32),
        compiler_params=_params("parallel"),
        name="attn_out_proj",
    )(*[a for a, _ in parts], o_real, o_meta, w_o)


def _mlp_kernel(h_ref, gain_ref, wu_ref, wd_ref, fgain_ref, out_ref, xn_ref, *, final_norm):
    k = pl.program_id(1)

    @pl.when(k == 0)
    def _():
        x = h_ref[...]
        xn_ref[...] = _rmsnorm(x, gain_ref[...]).astype(BF16)
        out_ref[...] = x

    u = jnp.dot(xn_ref[...], wu_ref[...], preferred_element_type=F32)
    a = jnp.square(jnp.maximum(u, 0.0)).astype(BF16)
    out_ref[...] += jnp.dot(a, wd_ref[...], preferred_element_type=F32)

    if final_norm:
        @pl.when(k == pl.num_programs(1) - 1)
        def _():
            out_ref[...] = _rmsnorm(out_ref[...], fgain_ref[...])


def _mlp_call(h, gain, w_up, w_down, layer, final_gain, first_tile, n_tiles, final_norm):
    _, d, dff = w_up.shape
    return pl.pallas_call(
        functools.partial(_mlp_kernel, final_norm=final_norm),
        grid=(n_tiles, dff // FF_TILE),
        in_specs=[pl.BlockSpec((ROW_TILE, d), lambda i, k: (first_tile + i, 0)), _vec_spec(d),
                  pl.BlockSpec((None, d, FF_TILE), lambda i, k: (layer, 0, k)),
                  pl.BlockSpec((None, FF_TILE, d), lambda i, k: (layer, k, 0)), _vec_spec(d)],
        out_specs=pl.BlockSpec((ROW_TILE, d), lambda i, k: (i, 0)),
        out_shape=jax.ShapeDtypeStruct((n_tiles * ROW_TILE, d), F32),
        scratch_shapes=[pltpu.VMEM((ROW_TILE, d), BF16)],
        compiler_params=_params("parallel", "arbitrary"),
        name="mlp_final" if final_norm else "mlp",
    )(h, gain, w_up, w_down, final_gain)


def _band_matrices(n, kp):
    r = lax.broadcasted_iota(jnp.int32, (n, kp), 0)
    c = lax.broadcasted_iota(jnp.int32, (n, kp), 1)
    off = c - r - HALO
    return [jnp.where((off >= -(w // 2)) & (off < w // 2), 1.0, 0.0).astype(BF16) for w in POOL_WINDOWS]


def _extended_rows(prev, cur, nxt, kp):
    pad = jnp.zeros((kp - cur.shape[0] - 2 * HALO, cur.shape[1]), F32)
    return jnp.concatenate([prev, cur, nxt, pad], axis=0).astype(BF16)


def _pool_rows(x, xn, xe_ref, band_ref, first_pos, seq_len, w_ref, scale_ref):
    n = x.shape[0]
    gdim = w_ref.shape[1]
    pos = first_pos + lax.broadcasted_iota(jnp.int32, (n, LANES), 0)
    outs = []
    for g, w in enumerate(POOL_WINDOWS):
        cols = slice(g * gdim, (g + 1) * gdim)
        window_sum = jnp.dot(band_ref[g], xe_ref[:, cols], preferred_element_type=F32)
        cnt = (jnp.minimum(pos + w // 2, seq_len) - jnp.maximum(pos - w // 2, 0)).astype(F32)
        inv = jnp.tile(1.0 / cnt, (1, gdim // LANES))
        mixed = (window_sum * inv - xn[:, cols]).astype(BF16)
        y = jnp.dot(mixed, w_ref[g], preferred_element_type=F32)
        outs.append(x[:, cols] + y * scale_ref[:, cols])
    return jnp.concatenate(outs, axis=-1)


def _pool_kernel(*refs, seq, n_real_tiles, with_tail):
    cur_ref, prev_ref, next_ref, gain_ref, w_ref, scale_ref = refs[:6]
    out_ref, xe_ref, band_ref = refs[-3:]
    i = pl.program_id(0)
    tiles = seq // ROW_TILE
    j = i % tiles

    @pl.when(i == 0)
    def _():
        for g, band in enumerate(_band_matrices(ROW_TILE, xe_ref.shape[0])):
            band_ref[g] = band

    @pl.when(i < n_real_tiles)
    def _():
        gain = gain_ref[...]
        x = cur_ref[...]
        xn = _rmsnorm(x, gain)
        nxt = jnp.where(j == tiles - 1, 0.0, _rmsnorm(next_ref[...], gain))
        xe_ref[...] = _extended_rows(_rmsnorm(prev_ref[...], gain), xn, nxt, xe_ref.shape[0])
        out_ref[...] = _pool_rows(x, xn, xe_ref, band_ref, N_META + j * ROW_TILE, N_META + seq, w_ref, scale_ref)

    if with_tail:
        @pl.when(i == n_real_tiles)
        def _():
            out_ref[...] = refs[6][...]


def _pool_meta_kernel(cur_ref, next_ref, gain_ref, w_ref, scale_ref, out_ref, xe_ref, band_ref, *, nseq, seq):
    b = pl.program_id(0)

    @pl.when(b == 0)
    def _():
        for g, band in enumerate(_band_matrices(N_META, xe_ref.shape[0])):
            band_ref[g] = band

    @pl.when(b < nseq)
    def _():
        gain = gain_ref[...]
        x = cur_ref[...]
        xn = _rmsnorm(x, gain)
        before = jnp.zeros((HALO, x.shape[1]), F32)
        xe_ref[...] = _extended_rows(before, xn, _rmsnorm(next_ref[...], gain), xe_ref.shape[0])
        out_ref[...] = _pool_rows(x, xn, xe_ref, band_ref, 0, N_META + seq, w_ref, scale_ref)

    @pl.when(b >= nseq)
    def _():
        out_ref[...] = jnp.zeros(out_ref.shape, F32)


def _pool_call(h, gain, w_pool, layer, scale, nseq, seq, n_real, with_meta):
    d = h.shape[1]
    tiles = seq // ROW_TILE
    n_real_tiles = n_real // ROW_TILE
    blk = ROW_TILE // HALO
    meta_blk0 = n_real // HALO
    w_spec = _layer_resident(w_pool, layer)
    in_specs, operands = [], []
    if with_meta:
        meta0 = n_real // N_META
        seq_of = lambda b: jnp.minimum(b, nseq - 1)
        tail = pl.pallas_call(
            functools.partial(_pool_meta_kernel, nseq=nseq, seq=seq),
            grid=(ROW_TILE // N_META,),
            in_specs=[pl.BlockSpec((N_META, d), lambda b: (meta0 + seq_of(b), 0)),
                      pl.BlockSpec((HALO, d), lambda b: (seq_of(b) * tiles * blk, 0)),
                      _vec_spec(d), w_spec, _vec_spec(d)],
            out_specs=pl.BlockSpec((N_META, d), lambda b: (b, 0)),
            out_shape=jax.ShapeDtypeStruct((ROW_TILE, d), F32),
            scratch_shapes=[pltpu.VMEM((LANES, d), BF16), pltpu.VMEM((len(POOL_WINDOWS), N_META, LANES), BF16)],
            compiler_params=_params("arbitrary"),
            name="pool_mixer_meta",
        )(h, h, gain, w_pool, scale)
        in_specs, operands = [pl.BlockSpec((ROW_TILE, d), lambda i: (0, 0))], [tail]

    real = lambda i: jnp.minimum(i, n_real_tiles - 1)
    cur_spec = pl.BlockSpec((ROW_TILE, d), lambda i: (real(i), 0))
    prev_spec = pl.BlockSpec((HALO, d), lambda i: (
        jnp.where(real(i) % tiles == 0, meta_blk0 + 2 * (real(i) // tiles) + 1, real(i) * blk - 1), 0))
    next_spec = pl.BlockSpec((HALO, d), lambda i: (jnp.where(real(i) % tiles == tiles - 1, 0, (real(i) + 1) * blk), 0))
    n_out_tiles = n_real_tiles + (1 if with_meta else 0)
    ext_rows = -(-(ROW_TILE + 2 * HALO) // LANES) * LANES
    return pl.pallas_call(
        functools.partial(_pool_kernel, seq=seq, n_real_tiles=n_real_tiles, with_tail=with_meta),
        grid=(n_out_tiles,),
        in_specs=[cur_spec, prev_spec, next_spec, _vec_spec(d), w_spec, _vec_spec(d)] + in_specs,
        out_specs=pl.BlockSpec((ROW_TILE, d), lambda i: (i, 0)),
        out_shape=jax.ShapeDtypeStruct((n_out_tiles * ROW_TILE, d), F32),
        scratch_shapes=[pltpu.VMEM((ext_rows, d), BF16), pltpu.VMEM((len(POOL_WINDOWS), ROW_TILE, ext_rows), BF16)],
        compiler_params=_params("arbitrary"),
        name="pool_mixer",
    )(h, h, h, gain, w_pool, scale, *operands)


def kernel(x_prompt, x_sample, meta_tokens, attn_norm, w_qkv, q_norm, k_norm, w_o, pool_norm, w_pool, pool_scale,
           mlp_norm, w_up, w_down, final_norm):
    b1, seq, d = x_prompt.shape
    b2 = x_sample.shape[0]
    assert x_sample.shape[1:] == (seq, d) and seq % ROW_TILE == 0 and seq % GRID_W == 0
    nseq = b1 + b2
    n_real = nseq * seq
    assert nseq * N_META <= ROW_TILE
    depth = mlp_norm.shape[0]
    t1, t2 = b1 * seq // ROW_TILE, b2 * seq // ROW_TILE

    tail = jnp.concatenate([jnp.tile(meta_tokens.astype(F32), (nseq, 1)),
                            jnp.zeros((ROW_TILE - nseq * N_META, d), F32)], axis=0)
    parts = [(x_prompt.reshape(b1 * seq, d), t1), (x_sample.reshape(b2 * seq, d), t2), (tail, 1)]

    nqk = (N_HEADS + N_KV_HEADS) * HEAD_DIM
    tables = _rope_tables(seq)
    w_qkv = jnp.concatenate([_split_pairs(w_qkv[..., :nqk], N_HEADS + N_KV_HEADS), w_qkv[..., nqk:]], axis=-1)
    w_qkv, w_o, w_pool, w_up, w_down = (w.astype(BF16) for w in (w_qkv, w_o, w_pool, w_up, w_down))
    row2 = lambda a: a.reshape(1, -1).astype(F32)
    q_gain = _split_pairs(q_norm.astype(F32), 1) * (HEAD_DIM ** -0.5 * math.log2(math.e))
    k_gain = _split_pairs(k_norm.astype(F32), 1)

    outs = None
    h = None
    for i in range(depth):
        j = i // 2
        last = i == depth - 1
        if i % 2 == 0:
            q, k, v = _qkv_call(parts, row2(attn_norm[j]), w_qkv, j, row2(q_gain[j]), row2(k_gain[j]), tables, seq, n_real)
            o_real, o_meta = _attn_call(q, k, v, nseq, seq, n_real)
            h = _oproj_call(parts, o_real, o_meta, w_o, j)
        else:
            h = _pool_call(h, row2(pool_norm[j]), w_pool, j, row2(pool_scale[j]), nseq, seq, n_real, with_meta=not last)
        if not last:
            h = _mlp_call(h, row2(mlp_norm[i]), w_up, w_down, i, row2(final_norm), 0, h.shape[0] // ROW_TILE, False)
            parts = [(h, h.shape[0] // ROW_TILE)]
        else:
            mlp = functools.partial(_mlp_call, h, row2(mlp_norm[i]), w_up, w_down, i, row2(final_norm))
            outs = (mlp(0, t1, True).reshape(b1, seq, d), mlp(t1, t2, True).reshape(b2, seq, d))
    return outs
```

```python
import functools
import math

import numpy as np
import jax
import jax.numpy as jnp
from jax import lax
from jax.experimental import pallas as pl
from jax.experimental.pallas import tpu as pltpu

N_HEADS = 16
N_KV_HEADS = 4
Q_PER_KV = N_HEADS // N_KV_HEADS
HEAD_DIM = 128
ROPE_THETA = 10000.0
GRID_W = 64
N_META = 16
POOL_WINDOWS = (2, 4, 8, 16)
EPS = 1e-6

LANES = 128
F32_SUBLANES = 8
HALO = max(POOL_WINDOWS) // 2
ROW_TILE = 512
FF_TILE = 1024
Q_TILE = 2048
MXU_COLS = 256
VMEM_LIMIT_BYTES = 56 * 1024 * 1024

assert HALO == F32_SUBLANES and N_META == 2 * HALO and HEAD_DIM == LANES

BF16 = jnp.bfloat16
F32 = jnp.float32


def _params(*semantics):
    return pltpu.CompilerParams(dimension_semantics=semantics, vmem_limit_bytes=VMEM_LIMIT_BYTES)


def _rmsnorm(x, gain):
    ms = jnp.mean(x * x, axis=-1, keepdims=True)
    return x * lax.rsqrt(ms + EPS) * gain


def _layer_resident(w, layer):
    zeros = (0,) * (w.ndim - 1)
    return pl.BlockSpec((None,) + w.shape[1:], lambda *_: (layer,) + zeros, pipeline_mode=pl.Buffered(1))


def _vec_spec(width):
    return pl.BlockSpec((1, width), lambda *_: (0, 0))


def _row_part_specs(parts, d):
    specs, start = [], 0
    for _, n in parts:
        index_map = functools.partial(lambda i, *_, start, n: (jnp.clip(i - start, 0, n - 1), 0), start=start, n=n)
        mode = dict(pipeline_mode=pl.Buffered(1)) if n == 1 and len(parts) > 1 else {}
        specs.append(pl.BlockSpec((ROW_TILE, d), index_map, **mode))
        start += n
    return specs


def _read_row_tile(refs, counts):
    i = pl.program_id(0)
    x = refs[0][...]
    start = 0
    for ref, n in zip(refs[1:], counts[:-1]):
        start += n
        x = jnp.where(i >= start, ref[...], x)
    return x


def _split_pairs(w, n_heads):
    lead = w.shape[:-1]
    return w.reshape(*lead, n_heads, HEAD_DIM // 2, 2).swapaxes(-1, -2).reshape(*lead, n_heads * HEAD_DIM)


def _rope_tables(seq):
    t = np.arange((seq // GRID_W) * GRID_W)
    r = (t // GRID_W).astype(np.float32)
    c = (t % GRID_W).astype(np.float32)
    axis_dim = HEAD_DIM // 2
    inv_freq = (ROPE_THETA ** (-np.arange(0, axis_dim, 2, dtype=np.float32) / axis_dim)).astype(np.float32)
    ang = np.concatenate([r[:, None] * inv_freq[None], c[:, None] * inv_freq[None]], axis=-1)
    cos, sin = np.cos(ang).astype(np.float32), np.sin(ang).astype(np.float32)
    cos = np.concatenate([cos, cos], axis=-1)
    sin = np.concatenate([-sin, sin], axis=-1)
    ident = np.ones((ROW_TILE, HEAD_DIM), np.float32)
    zero = np.zeros((ROW_TILE, HEAD_DIM), np.float32)
    return jnp.asarray(np.concatenate([cos, ident])), jnp.asarray(np.concatenate([sin, zero]))


def _qkv_kernel(*refs, counts):
    n = len(counts)
    gain_ref, w_ref, qg_ref, kg_ref, cos_ref, sin_ref, q_ref, k_ref, v_ref, y_ref = refs[n:]
    i = pl.program_id(0)
    n_tiles = pl.num_programs(0) - 1
    nq, nk = q_ref.shape[1], k_ref.shape[1]

    def finish(col):
        gain = qg_ref[...] if col < nq else kg_ref[...]
        yn = _rmsnorm(y_ref[:, col:col + HEAD_DIM], gain)
        out = (yn * cos_ref[...] + pltpu.roll(yn, HEAD_DIM // 2, 1) * sin_ref[...]).astype(BF16)
        if col < nq:
            q_ref[:, col:col + HEAD_DIM] = out
        else:
            k_ref[:, col - nq:col - nq + HEAD_DIM] = out

    def step(project, finish_previous):
        if project:
            xn = _rmsnorm(_read_row_tile(refs[:n], counts), gain_ref[...]).astype(BF16)
        for c0 in range(0, nq + 2 * nk, MXU_COLS):
            head_cols = range(c0, c0 + MXU_COLS, HEAD_DIM)
            if finish_previous:
                for col in head_cols:
                    if col < nq + nk:
                        finish(col)
            if project:
                y = jnp.dot(xn, w_ref[:, c0:c0 + MXU_COLS], preferred_element_type=F32)
                for col in head_cols:
                    yh = y[:, col - c0:col - c0 + HEAD_DIM]
                    if col < nq + nk:
                        y_ref[:, col:col + HEAD_DIM] = yh
                    else:
                        v_ref[:, col - nq - nk:col - nq - nk + HEAD_DIM] = yh.astype(BF16)

    pl.when(i == 0)(functools.partial(step, True, False))
    pl.when((i > 0) & (i < n_tiles))(functools.partial(step, True, True))
    pl.when(i == n_tiles)(functools.partial(step, False, True))


def _qkv_call(parts, gain, w_qkv, layer, q_gain, k_gain, tables, seq, n_real):
    d = w_qkv.shape[1]
    counts = tuple(n for _, n in parts)
    n_tiles = sum(counts)
    ntok = n_tiles * ROW_TILE
    nq, nk = N_HEADS * HEAD_DIM, N_KV_HEADS * HEAD_DIM
    tiles_per_seq, n_real_tiles = seq // ROW_TILE, n_real // ROW_TILE
    finished = lambda i: jnp.maximum(i - 1, 0)
    row = lambda width, tile: pl.BlockSpec((ROW_TILE, width), lambda i: (tile(i), 0))
    table = pl.BlockSpec((ROW_TILE, HEAD_DIM), lambda i: (
        jnp.where(finished(i) < n_real_tiles, finished(i) % tiles_per_seq, tiles_per_seq), 0))
    return pl.pallas_call(
        functools.partial(_qkv_kernel, counts=counts),
        grid=(n_tiles + 1,),
        in_specs=_row_part_specs(parts, d) + [_vec_spec(d), _layer_resident(w_qkv, layer), _vec_spec(HEAD_DIM),
                                              _vec_spec(HEAD_DIM), table, table],
        out_specs=[row(nq, finished), row(nk, finished), row(nk, lambda i: jnp.minimum(i, n_tiles - 1))],
        out_shape=[jax.ShapeDtypeStruct((ntok, nq), BF16),
                   jax.ShapeDtypeStruct((ntok, nk), BF16),
                   jax.ShapeDtypeStruct((ntok, nk), BF16)],
        scratch_shapes=[pltpu.VMEM((ROW_TILE, nq + nk), F32)],
        compiler_params=_params("arbitrary"),
        name="qkv_proj",
    )(*[a for a, _ in parts], gain, w_qkv, q_gain, k_gain, *tables)


def _attn_kernel(q_ref, qm_ref, k_ref, km_ref, v_ref, vm_ref, zeros_ref, o_ref, om_ref, vt_ref, vtpad_ref):
    del zeros_ref
    first_q_tile = pl.program_id(2) == 0

    @pl.when(first_q_tile)
    def _():
        pad = jnp.zeros((LANES - N_META, HEAD_DIM), BF16)
        vt_ref[...] = v_ref[...].astype(F32).T.astype(BF16)
        vtpad_ref[...] = jnp.concatenate([vm_ref[...], pad], axis=0).astype(F32).T.astype(BF16)

    def scores(qh):
        contract_last = (((1,), (1,)), ((), ()))
        st = lax.dot_general(k_ref[...], qh, contract_last, preferred_element_type=F32)
        stm = lax.dot_general(km_ref[...], qh, contract_last, preferred_element_type=F32)
        return st, stm

    def softmax_pv(st, stm):
        m = jnp.maximum(jnp.max(st, axis=0, keepdims=True), jnp.max(stm, axis=0, keepdims=True))
        p, pm = jnp.exp2(st - m), jnp.exp2(stm - m)
        denom = jnp.sum(p, axis=0, keepdims=True) + jnp.sum(pm, axis=0, keepdims=True)
        pm = jnp.concatenate([pm.astype(BF16), jnp.zeros((LANES - N_META, pm.shape[1]), BF16)], axis=0)
        ot = (jnp.dot(vt_ref[...], p.astype(BF16), preferred_element_type=F32)
              + jnp.dot(vtpad_ref[...], pm, preferred_element_type=F32))
        return (ot * (1.0 / denom)).T

    def attend(qh):
        return softmax_pv(*scores(qh))

    for hq in range(Q_PER_KV):
        cols = slice(hq * HEAD_DIM, (hq + 1) * HEAD_DIM)
        o_ref[:, cols] = attend(q_ref[:, cols]).astype(BF16)

    @pl.when(first_q_tile)
    def _():
        qm = [qm_ref[:, hq * HEAD_DIM:(hq + 1) * HEAD_DIM] for hq in range(Q_PER_KV)]
        qm.append(jnp.zeros((LANES - Q_PER_KV * N_META, HEAD_DIM), BF16))
        om = attend(jnp.concatenate(qm, axis=0))
        for hq in range(Q_PER_KV):
            om_ref[:, hq * HEAD_DIM:(hq + 1) * HEAD_DIM] = om[hq * N_META:(hq + 1) * N_META].astype(BF16)


def _attn_call(q, k, v, nseq, seq, n_real):
    gw = Q_PER_KV * HEAD_DIM
    q_tiles = seq // Q_TILE
    meta0 = n_real // N_META
    q_spec = pl.BlockSpec((Q_TILE, gw), lambda b, g, i: (b * q_tiles + i, g))
    qm_spec = pl.BlockSpec((N_META, gw), lambda b, g, i: (meta0 + b, g))
    kv_spec = pl.BlockSpec((seq, HEAD_DIM), lambda b, g, i: (b, g))
    kvm_spec = pl.BlockSpec((N_META, HEAD_DIM), lambda b, g, i: (meta0 + b, g))
    om_spec = pl.BlockSpec((N_META, gw), lambda b, g, i: (b, g))
    zeros = jnp.zeros((ROW_TILE, N_HEADS * HEAD_DIM), BF16)
    return pl.pallas_call(
        _attn_kernel,
        grid=(nseq, N_KV_HEADS, q_tiles),
        in_specs=[q_spec, qm_spec, kv_spec, kvm_spec, kv_spec, kvm_spec, pl.BlockSpec(memory_space=pl.ANY)],
        out_specs=[q_spec, om_spec],
        out_shape=[jax.ShapeDtypeStruct((n_real, N_HEADS * HEAD_DIM), BF16),
                   jax.ShapeDtypeStruct(zeros.shape, BF16)],
        scratch_shapes=[pltpu.VMEM((HEAD_DIM, seq), BF16), pltpu.VMEM((HEAD_DIM, LANES), BF16)],
        input_output_aliases={6: 1},
        compiler_params=_params("parallel", "parallel", "arbitrary"),
        name="attention",
    )(q, q, k, k, v, v, zeros)


def _oproj_kernel(*refs, counts):
    n = len(counts)
    o_ref, om_ref, w_ref, gain_ref, out_ref, xn_ref = refs[n:]
    is_meta_tile = pl.program_id(0) == pl.num_programs(0) - 1
    o = jnp.where(is_meta_tile, om_ref[...], o_ref[...])
    y = _read_row_tile(refs[:n], counts) + jnp.dot(o, w_ref[...], preferred_element_type=F32)
    out_ref[...] = y
    xn_ref[...] = _rmsnorm(y, gain_ref[...]).astype(BF16)


def _oproj_call(parts, o_real, o_meta, w_o, layer, next_gain):
    d = w_o.shape[2]
    counts = tuple(n for _, n in parts)
    n_tiles = sum(counts)
    n_real_tiles = o_real.shape[0] // ROW_TILE
    o_spec = pl.BlockSpec((ROW_TILE, o_real.shape[1]), lambda i: (jnp.minimum(i, n_real_tiles - 1), 0))
    om_spec = pl.BlockSpec((ROW_TILE, o_real.shape[1]), lambda i: (0, 0))
    return pl.pallas_call(
        functools.partial(_oproj_kernel, counts=counts),
        grid=(n_tiles,),
        in_specs=_row_part_specs(parts, d) + [o_spec, om_spec, _layer_resident(w_o, layer), _vec_spec(d)],
        out_specs=[pl.BlockSpec((ROW_TILE, d), lambda i: (i, 0))] * 2,
        out_shape=[jax.ShapeDtypeStruct((n_tiles * ROW_TILE, d), F32),
                   jax.ShapeDtypeStruct((n_tiles * ROW_TILE, d), BF16)],
        compiler_params=_params("parallel"),
        name="attn_out_proj",
    )(*[a for a, _ in parts], o_real, o_meta, w_o, next_gain)


def _mlp_kernel(h_ref, gain_ref, wu_ref, wd_ref, fgain_ref, *rest, final_norm, pre_normed):
    xn_ref, out_ref = rest if pre_normed else rest[::-1]
    k = pl.program_id(1)

    @pl.when(k == 0)
    def _():
        x = h_ref[...]
        if not pre_normed:
            xn_ref[...] = _rmsnorm(x, gain_ref[...]).astype(BF16)
        out_ref[...] = x

    u = jnp.dot(xn_ref[...], wu_ref[...], preferred_element_type=F32)
    a = jnp.square(jnp.maximum(u, 0.0)).astype(BF16)
    out_ref[...] += jnp.dot(a, wd_ref[...], preferred_element_type=F32)

    if final_norm:
        @pl.when(k == pl.num_programs(1) - 1)
        def _():
            out_ref[...] = _rmsnorm(out_ref[...], fgain_ref[...])


def _mlp_call(h, gain, w_up, w_down, layer, final_gain, first_tile, n_tiles, final_norm, xn=None):
    _, d, dff = w_up.shape
    row = pl.BlockSpec((ROW_TILE, d), lambda i, k: (first_tile + i, 0))
    pre = xn is not None
    return pl.pallas_call(
        functools.partial(_mlp_kernel, final_norm=final_norm, pre_normed=pre),
        grid=(n_tiles, dff // FF_TILE),
        in_specs=[row, _vec_spec(d),
                  pl.BlockSpec((None, d, FF_TILE), lambda i, k: (layer, 0, k)),
                  pl.BlockSpec((None, FF_TILE, d), lambda i, k: (layer, k, 0)), _vec_spec(d)] + ([row] if pre else []),
        out_specs=pl.BlockSpec((ROW_TILE, d), lambda i, k: (i, 0)),
        out_shape=jax.ShapeDtypeStruct((n_tiles * ROW_TILE, d), F32),
        scratch_shapes=[] if pre else [pltpu.VMEM((ROW_TILE, d), BF16)],
        compiler_params=_params("parallel", "arbitrary"),
        name="mlp_final" if final_norm else "mlp",
    )(h, gain, w_up, w_down, final_gain, *([xn] if pre else []))


def _band_matrices(n, kp):
    r = lax.broadcasted_iota(jnp.int32, (n, kp), 0)
    c = lax.broadcasted_iota(jnp.int32, (n, kp), 1)
    off = c - r - HALO
    return [jnp.where((off >= -(w // 2)) & (off < w // 2), 1.0, 0.0).astype(BF16) for w in POOL_WINDOWS]


def _extended_rows(prev, cur, nxt, kp):
    pad = jnp.zeros((kp - cur.shape[0] - 2 * HALO, cur.shape[1]), F32)
    return jnp.concatenate([prev, cur, nxt, pad], axis=0).astype(BF16)


def _pool_rows(x, xn, xe_ref, band_ref, first_pos, seq_len, w_ref, scale_ref):
    n = x.shape[0]
    gdim = w_ref.shape[1]
    pos = first_pos + lax.broadcasted_iota(jnp.int32, (n, LANES), 0)
    outs = []
    for g, w in enumerate(POOL_WINDOWS):
        cols = slice(g * gdim, (g + 1) * gdim)
        window_sum = jnp.dot(band_ref[g], xe_ref[:, cols], preferred_element_type=F32)
        cnt = (jnp.minimum(pos + w // 2, seq_len) - jnp.maximum(pos - w // 2, 0)).astype(F32)
        inv = jnp.tile(1.0 / cnt, (1, gdim // LANES))
        mixed = (window_sum * inv - xn[:, cols]).astype(BF16)
        y = jnp.dot(mixed, w_ref[g], preferred_element_type=F32)
        outs.append(x[:, cols] + y * scale_ref[:, cols])
    return jnp.concatenate(outs, axis=-1)


def _pool_kernel(*refs, seq, n_real_tiles, with_tail):
    cur_ref, prev_ref, next_ref, gain_ref, w_ref, scale_ref = refs[:6]
    out_ref, xe_ref, band_ref = refs[-3:]
    i = pl.program_id(0)
    tiles = seq // ROW_TILE
    j = i % tiles

    @pl.when(i == 0)
    def _():
        for g, band in enumerate(_band_matrices(ROW_TILE, xe_ref.shape[0])):
            band_ref[g] = band

    @pl.when(i < n_real_tiles)
    def _():
        gain = gain_ref[...]
        x = cur_ref[...]
        xn = _rmsnorm(x, gain)
        nxt = jnp.where(j == tiles - 1, 0.0, _rmsnorm(next_ref[...], gain))
        xe_ref[...] = _extended_rows(_rmsnorm(prev_ref[...], gain), xn, nxt, xe_ref.shape[0])
        out_ref[...] = _pool_rows(x, xn, xe_ref, band_ref, N_META + j * ROW_TILE, N_META + seq, w_ref, scale_ref)

    if with_tail:
        @pl.when(i == n_real_tiles)
        def _():
            out_ref[...] = refs[6][...]


def _pool_meta_kernel(cur_ref, next_ref, gain_ref, w_ref, scale_ref, out_ref, xe_ref, band_ref, *, nseq, seq):
    b = pl.program_id(0)

    @pl.when(b == 0)
    def _():
        for g, band in enumerate(_band_matrices(N_META, xe_ref.shape[0])):
            band_ref[g] = band

    @pl.when(b < nseq)
    def _():
        gain = gain_ref[...]
        x = cur_ref[...]
        xn = _rmsnorm(x, gain)
        before = jnp.zeros((HALO, x.shape[1]), F32)
        xe_ref[...] = _extended_rows(before, xn, _rmsnorm(next_ref[...], gain), xe_ref.shape[0])
        out_ref[...] = _pool_rows(x, xn, xe_ref, band_ref, 0, N_META + seq, w_ref, scale_ref)

    @pl.when(b >= nseq)
    def _():
        out_ref[...] = jnp.zeros(out_ref.shape, F32)


def _pool_call(h, gain, w_pool, layer, scale, nseq, seq, n_real, with_meta):
    d = h.shape[1]
    tiles = seq // ROW_TILE
    n_real_tiles = n_real // ROW_TILE
    blk = ROW_TILE // HALO
    meta_blk0 = n_real // HALO
    w_spec = _layer_resident(w_pool, layer)
    in_specs, operands = [], []
    if with_meta:
        meta0 = n_real // N_META
        seq_of = lambda b: jnp.minimum(b, nseq - 1)
        tail = pl.pallas_call(
            functools.partial(_pool_meta_kernel, nseq=nseq, seq=seq),
            grid=(ROW_TILE // N_META,),
            in_specs=[pl.BlockSpec((N_META, d), lambda b: (meta0 + seq_of(b), 0)),
                      pl.BlockSpec((HALO, d), lambda b: (seq_of(b) * tiles * blk, 0)),
                      _vec_spec(d), w_spec, _vec_spec(d)],
            out_specs=pl.BlockSpec((N_META, d), lambda b: (b, 0)),
            out_shape=jax.ShapeDtypeStruct((ROW_TILE, d), F32),
            scratch_shapes=[pltpu.VMEM((LANES, d), BF16), pltpu.VMEM((len(POOL_WINDOWS), N_META, LANES), BF16)],
            compiler_params=_params("arbitrary"),
            name="pool_mixer_meta",
        )(h, h, gain, w_pool, scale)
        in_specs, operands = [pl.BlockSpec((ROW_TILE, d), lambda i: (0, 0))], [tail]

    real = lambda i: jnp.minimum(i, n_real_tiles - 1)
    cur_spec = pl.BlockSpec((ROW_TILE, d), lambda i: (real(i), 0))
    prev_spec = pl.BlockSpec((HALO, d), lambda i: (
        jnp.where(real(i) % tiles == 0, meta_blk0 + 2 * (real(i) // tiles) + 1, real(i) * blk - 1), 0))
    next_spec = pl.BlockSpec((HALO, d), lambda i: (jnp.where(real(i) % tiles == tiles - 1, 0, (real(i) + 1) * blk), 0))
    n_out_tiles = n_real_tiles + (1 if with_meta else 0)
    ext_rows = -(-(ROW_TILE + 2 * HALO) // LANES) * LANES
    return pl.pallas_call(
        functools.partial(_pool_kernel, seq=seq, n_real_tiles=n_real_tiles, with_tail=with_meta),
        grid=(n_out_tiles,),
        in_specs=[cur_spec, prev_spec, next_spec, _vec_spec(d), w_spec, _vec_spec(d)] + in_specs,
        out_specs=pl.BlockSpec((ROW_TILE, d), lambda i: (i, 0)),
        out_shape=jax.ShapeDtypeStruct((n_out_tiles * ROW_TILE, d), F32),
        scratch_shapes=[pltpu.VMEM((ext_rows, d), BF16), pltpu.VMEM((len(POOL_WINDOWS), ROW_TILE, ext_rows), BF16)],
        compiler_params=_params("arbitrary"),
        name="pool_mixer",
    )(h, h, h, gain, w_pool, scale, *operands)


def kernel(x_prompt, x_sample, meta_tokens, attn_norm, w_qkv, q_norm, k_norm, w_o, pool_norm, w_pool, pool_scale,
           mlp_norm, w_up, w_down, final_norm):
    b1, seq, d = x_prompt.shape
    b2 = x_sample.shape[0]
    assert x_sample.shape[1:] == (seq, d) and seq % ROW_TILE == 0 and seq % GRID_W == 0
    nseq = b1 + b2
    n_real = nseq * seq
    assert nseq * N_META <= ROW_TILE
    depth = mlp_norm.shape[0]
    t1, t2 = b1 * seq // ROW_TILE, b2 * seq // ROW_TILE

    tail = jnp.concatenate([jnp.tile(meta_tokens.astype(F32), (nseq, 1)),
                            jnp.zeros((ROW_TILE - nseq * N_META, d), F32)], axis=0)
    parts = [(x_prompt.reshape(b1 * seq, d), t1), (x_sample.reshape(b2 * seq, d), t2), (tail, 1)]

    nqk = (N_HEADS + N_KV_HEADS) * HEAD_DIM
    tables = _rope_tables(seq)
    w_qkv = jnp.concatenate([_split_pairs(w_qkv[..., :nqk], N_HEADS + N_KV_HEADS), w_qkv[..., nqk:]], axis=-1)
    w_qkv, w_o, w_pool, w_up, w_down = (w.astype(BF16) for w in (w_qkv, w_o, w_pool, w_up, w_down))
    row2 = lambda a: a.reshape(1, -1).astype(F32)
    q_gain = _split_pairs(q_norm.astype(F32), 1) * (HEAD_DIM ** -0.5 * math.log2(math.e))
    k_gain = _split_pairs(k_norm.astype(F32), 1)

    outs = None
    h = None
    for i in range(depth):
        j = i // 2
        last = i == depth - 1
        if i % 2 == 0:
            q, k, v = _qkv_call(parts, row2(attn_norm[j]), w_qkv, j, row2(q_gain[j]), row2(k_gain[j]), tables, seq, n_real)
            o_real, o_meta = _attn_call(q, k, v, nseq, seq, n_real)
            h, xn = _oproj_call(parts, o_real, o_meta, w_o, j, row2(mlp_norm[i]))
        else:
            h = _pool_call(h, row2(pool_norm[j]), w_pool, j, row2(pool_scale[j]), nseq, seq, n_real, with_meta=not last)
            xn = None
        if not last:
            h = _mlp_call(h, row2(mlp_norm[i]), w_up, w_down, i, row2(final_norm), 0, h.shape[0] // ROW_TILE, False, xn)
            parts = [(h, h.shape[0] // ROW_TILE)]
        else:
            mlp = functools.partial(_mlp_call, h, row2(mlp_norm[i]), w_up, w_down, i, row2(final_norm))
            outs = (mlp(0, t1, True).reshape(b1, seq, d), mlp(t1, t2, True).reshape(b2, seq, d))
    return outs
```

```python
import functools
import math

import numpy as np
import jax
import jax.numpy as jnp
from jax import lax
from jax.experimental import pallas as pl
from jax.experimental.pallas import tpu as pltpu

N_HEADS = 16
N_KV_HEADS = 4
Q_PER_KV = N_HEADS // N_KV_HEADS
HEAD_DIM = 128
ROPE_THETA = 10000.0
GRID_W = 64
N_META = 16
POOL_WINDOWS = (2, 4, 8, 16)
EPS = 1e-6

LANES = 128
F32_SUBLANES = 8
HALO = max(POOL_WINDOWS) // 2
ROW_TILE = 512
FF_TILE = 1024
Q_TILE = 2048
MXU_COLS = 256
VMEM_LIMIT_BYTES = 56 * 1024 * 1024

assert HALO == F32_SUBLANES and N_META == 2 * HALO and HEAD_DIM == LANES

BF16 = jnp.bfloat16
F32 = jnp.float32


def _params(*semantics):
    return pltpu.CompilerParams(dimension_semantics=semantics, vmem_limit_bytes=VMEM_LIMIT_BYTES)


def _rmsnorm(x, gain):
    ms = jnp.mean(x * x, axis=-1, keepdims=True)
    return x * lax.rsqrt(ms + EPS) * gain


def _layer_resident(w, layer):
    zeros = (0,) * (w.ndim - 1)
    return pl.BlockSpec((None,) + w.shape[1:], lambda *_: (layer,) + zeros, pipeline_mode=pl.Buffered(1))


def _vec_spec(width):
    return pl.BlockSpec((1, width), lambda *_: (0, 0))


def _row_part_specs(parts, d):
    specs, start = [], 0
    for _, n in parts:
        index_map = functools.partial(lambda i, *_, start, n: (jnp.clip(i - start, 0, n - 1), 0), start=start, n=n)
        mode = dict(pipeline_mode=pl.Buffered(1)) if n == 1 and len(parts) > 1 else {}
        specs.append(pl.BlockSpec((ROW_TILE, d), index_map, **mode))
        start += n
    return specs


def _read_row_tile(refs, counts):
    i = pl.program_id(0)
    x = refs[0][...]
    start = 0
    for ref, n in zip(refs[1:], counts[:-1]):
        start += n
        x = jnp.where(i >= start, ref[...], x)
    return x


def _split_pairs(w, n_heads):
    lead = w.shape[:-1]
    return w.reshape(*lead, n_heads, HEAD_DIM // 2, 2).swapaxes(-1, -2).reshape(*lead, n_heads * HEAD_DIM)


def _rope_tables(seq):
    t = np.arange((seq // GRID_W) * GRID_W)
    r = (t // GRID_W).astype(np.float32)
    c = (t % GRID_W).astype(np.float32)
    axis_dim = HEAD_DIM // 2
    inv_freq = (ROPE_THETA ** (-np.arange(0, axis_dim, 2, dtype=np.float32) / axis_dim)).astype(np.float32)
    ang = np.concatenate([r[:, None] * inv_freq[None], c[:, None] * inv_freq[None]], axis=-1)
    cos, sin = np.cos(ang).astype(np.float32), np.sin(ang).astype(np.float32)
    cos = np.concatenate([cos, cos], axis=-1)
    sin = np.concatenate([-sin, sin], axis=-1)
    ident = np.ones((ROW_TILE, HEAD_DIM), np.float32)
    zero = np.zeros((ROW_TILE, HEAD_DIM), np.float32)
    return jnp.asarray(np.concatenate([cos, ident])), jnp.asarray(np.concatenate([sin, zero]))


def _qkv_kernel(*refs, counts):
    n = len(counts)
    gain_ref, w_ref, qg_ref, kg_ref, cos_ref, sin_ref, q_ref, k_ref, v_ref, y_ref = refs[n:]
    i = pl.program_id(0)
    n_tiles = pl.num_programs(0) - 1
    nq, nk = q_ref.shape[1], k_ref.shape[1]

    def finish(col):
        gain = qg_ref[...] if col < nq else kg_ref[...]
        yn = _rmsnorm(y_ref[:, col:col + HEAD_DIM], gain)
        out = (yn * cos_ref[...] + pltpu.roll(yn, HEAD_DIM // 2, 1) * sin_ref[...]).astype(BF16)
        if col < nq:
            q_ref[:, col:col + HEAD_DIM] = out
        else:
            k_ref[:, col - nq:col - nq + HEAD_DIM] = out

    def step(project, finish_previous):
        if project:
            xn = _rmsnorm(_read_row_tile(refs[:n], counts), gain_ref[...]).astype(BF16)
        for c0 in range(0, nq + 2 * nk, MXU_COLS):
            head_cols = range(c0, c0 + MXU_COLS, HEAD_DIM)
            if finish_previous:
                for col in head_cols:
                    if col < nq + nk:
                        finish(col)
            if project:
                y = jnp.dot(xn, w_ref[:, c0:c0 + MXU_COLS], preferred_element_type=F32)
                for col in head_cols:
                    yh = y[:, col - c0:col - c0 + HEAD_DIM]
                    if col < nq + nk:
                        y_ref[:, col:col + HEAD_DIM] = yh
                    else:
                        v_ref[:, col - nq - nk:col - nq - nk + HEAD_DIM] = yh.astype(BF16)

    pl.when(i == 0)(functools.partial(step, True, False))
    pl.when((i > 0) & (i < n_tiles))(functools.partial(step, True, True))
    pl.when(i == n_tiles)(functools.partial(step, False, True))


def _qkv_call(parts, gain, w_qkv, layer, q_gain, k_gain, tables, seq, n_real):
    d = w_qkv.shape[1]
    counts = tuple(n for _, n in parts)
    n_tiles = sum(counts)
    ntok = n_tiles * ROW_TILE
    nq, nk = N_HEADS * HEAD_DIM, N_KV_HEADS * HEAD_DIM
    tiles_per_seq, n_real_tiles = seq // ROW_TILE, n_real // ROW_TILE
    finished = lambda i: jnp.maximum(i - 1, 0)
    row = lambda width, tile: pl.BlockSpec((ROW_TILE, width), lambda i: (tile(i), 0))
    table = pl.BlockSpec((ROW_TILE, HEAD_DIM), lambda i: (
        jnp.where(finished(i) < n_real_tiles, finished(i) % tiles_per_seq, tiles_per_seq), 0))
    return pl.pallas_call(
        functools.partial(_qkv_kernel, counts=counts),
        grid=(n_tiles + 1,),
        in_specs=_row_part_specs(parts, d) + [_vec_spec(d), _layer_resident(w_qkv, layer), _vec_spec(HEAD_DIM),
                                              _vec_spec(HEAD_DIM), table, table],
        out_specs=[row(nq, finished), row(nk, finished), row(nk, lambda i: jnp.minimum(i, n_tiles - 1))],
        out_shape=[jax.ShapeDtypeStruct((ntok, nq), BF16),
                   jax.ShapeDtypeStruct((ntok, nk), BF16),
                   jax.ShapeDtypeStruct((ntok, nk), BF16)],
        scratch_shapes=[pltpu.VMEM((ROW_TILE, nq + nk), F32)],
        compiler_params=_params("arbitrary"),
        name="qkv_proj",
    )(*[a for a, _ in parts], gain, w_qkv, q_gain, k_gain, *tables)


def _attn_kernel(q_ref, qm_ref, k_ref, km_ref, v_ref, vm_ref, zeros_ref, o_ref, om_ref, vt_ref, vtpad_ref):
    del zeros_ref
    first_q_tile = pl.program_id(2) == 0

    @pl.when(first_q_tile)
    def _():
        pad = jnp.zeros((LANES - N_META, HEAD_DIM), BF16)
        vt_ref[...] = v_ref[...].astype(F32).T.astype(BF16)
        vtpad_ref[...] = jnp.concatenate([vm_ref[...], pad], axis=0).astype(F32).T.astype(BF16)

    def scores(qh):
        contract_last = (((1,), (1,)), ((), ()))
        st = lax.dot_general(k_ref[...], qh, contract_last, preferred_element_type=F32)
        stm = lax.dot_general(km_ref[...], qh, contract_last, preferred_element_type=F32)
        return st, stm

    def softmax_pv(st, stm):
        m = jnp.maximum(jnp.max(st, axis=0, keepdims=True), jnp.max(stm, axis=0, keepdims=True))
        p, pm = jnp.exp2(st - m), jnp.exp2(stm - m)
        denom = jnp.sum(p, axis=0, keepdims=True) + jnp.sum(pm, axis=0, keepdims=True)
        pm = jnp.concatenate([pm.astype(BF16), jnp.zeros((LANES - N_META, pm.shape[1]), BF16)], axis=0)
        ot = (jnp.dot(vt_ref[...], p.astype(BF16), preferred_element_type=F32)
              + jnp.dot(vtpad_ref[...], pm, preferred_element_type=F32))
        return (ot * (1.0 / denom)).T

    def attend(qh):
        return softmax_pv(*scores(qh))

    for h0 in range(0, Q_PER_KV, 2):
        pair = [slice(hq * HEAD_DIM, (hq + 1) * HEAD_DIM) for hq in (h0, h0 + 1)]
        pair_scores = [scores(q_ref[:, cols]) for cols in pair]
        for cols, s in zip(pair, pair_scores):
            o_ref[:, cols] = softmax_pv(*s).astype(BF16)

    @pl.when(first_q_tile)
    def _():
        qm = [qm_ref[:, hq * HEAD_DIM:(hq + 1) * HEAD_DIM] for hq in range(Q_PER_KV)]
        qm.append(jnp.zeros((LANES - Q_PER_KV * N_META, HEAD_DIM), BF16))
        om = attend(jnp.concatenate(qm, axis=0))
        for hq in range(Q_PER_KV):
            om_ref[:, hq * HEAD_DIM:(hq + 1) * HEAD_DIM] = om[hq * N_META:(hq + 1) * N_META].astype(BF16)


def _attn_call(q, k, v, nseq, seq, n_real):
    gw = Q_PER_KV * HEAD_DIM
    q_tiles = seq // Q_TILE
    meta0 = n_real // N_META
    q_spec = pl.BlockSpec((Q_TILE, gw), lambda b, g, i: (b * q_tiles + i, g))
    qm_spec = pl.BlockSpec((N_META, gw), lambda b, g, i: (meta0 + b, g))
    kv_spec = pl.BlockSpec((seq, HEAD_DIM), lambda b, g, i: (b, g))
    kvm_spec = pl.BlockSpec((N_META, HEAD_DIM), lambda b, g, i: (meta0 + b, g))
    om_spec = pl.BlockSpec((N_META, gw), lambda b, g, i: (b, g))
    zeros = jnp.zeros((ROW_TILE, N_HEADS * HEAD_DIM), BF16)
    return pl.pallas_call(
        _attn_kernel,
        grid=(nseq, N_KV_HEADS, q_tiles),
        in_specs=[q_spec, qm_spec, kv_spec, kvm_spec, kv_spec, kvm_spec, pl.BlockSpec(memory_space=pl.ANY)],
        out_specs=[q_spec, om_spec],
        out_shape=[jax.ShapeDtypeStruct((n_real, N_HEADS * HEAD_DIM), BF16),
                   jax.ShapeDtypeStruct(zeros.shape, BF16)],
        scratch_shapes=[pltpu.VMEM((HEAD_DIM, seq), BF16), pltpu.VMEM((HEAD_DIM, LANES), BF16)],
        input_output_aliases={6: 1},
        compiler_params=_params("parallel", "parallel", "arbitrary"),
        name="attention",
    )(q, q, k, k, v, v, zeros)


def _oproj_kernel(*refs, counts):
    n = len(counts)
    o_ref, om_ref, w_ref, out_ref = refs[n:]
    is_meta_tile = pl.program_id(0) == pl.num_programs(0) - 1
    o = jnp.where(is_meta_tile, om_ref[...], o_ref[...])
    out_ref[...] = _read_row_tile(refs[:n], counts) + jnp.dot(o, w_ref[...], preferred_element_type=F32)


def _oproj_call(parts, o_real, o_meta, w_o, layer):
    d = w_o.shape[2]
    counts = tuple(n for _, n in parts)
    n_tiles = sum(counts)
    n_real_tiles = o_real.shape[0] // ROW_TILE
    o_spec = pl.BlockSpec((ROW_TILE, o_real.shape[1]), lambda i: (jnp.minimum(i, n_real_tiles - 1), 0))
    om_spec = pl.BlockSpec((ROW_TILE, o_real.shape[1]), lambda i: (0, 0))
    return pl.pallas_call(
        functools.partial(_oproj_kernel, counts=counts),
        grid=(n_tiles,),
        in_specs=_row_part_specs(parts, d) + [o_spec, om_spec, _layer_resident(w_o, layer)],
        out_specs=pl.BlockSpec((ROW_TILE, d), lambda i: (i, 0)),
        out_shape=jax.ShapeDtypeStruct((n_tiles * ROW_TILE, d), F32),
        compiler_params=_params("parallel"),
        name="attn_out_proj",
    )(*[a for a, _ in parts], o_real, o_meta, w_o)


def _mlp_kernel(h_ref, gain_ref, wu_ref, wd_ref, fgain_ref, out_ref, xn_ref, *, final_norm):
    k = pl.program_id(1)

    @pl.when(k == 0)
    def _():
        x = h_ref[...]
        xn_ref[...] = _rmsnorm(x, gain_ref[...]).astype(BF16)
        out_ref[...] = x

    u = jnp.dot(xn_ref[...], wu_ref[...], preferred_element_type=F32)
    a = jnp.square(jnp.maximum(u, 0.0)).astype(BF16)
    out_ref[...] += jnp.dot(a, wd_ref[...], preferred_element_type=F32)

    if final_norm:
        @pl.when(k == pl.num_programs(1) - 1)
        def _():
            out_ref[...] = _rmsnorm(out_ref[...], fgain_ref[...])


def _mlp_call(h, gain, w_up, w_down, layer, final_gain, first_tile, n_tiles, final_norm):
    _, d, dff = w_up.shape
    return pl.pallas_call(
        functools.partial(_mlp_kernel, final_norm=final_norm),
        grid=(n_tiles, dff // FF_TILE),
        in_specs=[pl.BlockSpec((ROW_TILE, d), lambda i, k: (first_tile + i, 0)), _vec_spec(d),
                  pl.BlockSpec((None, d, FF_TILE), lambda i, k: (layer, 0, k)),
                  pl.BlockSpec((None, FF_TILE, d), lambda i, k: (layer, k, 0)), _vec_spec(d)],
        out_specs=pl.BlockSpec((ROW_TILE, d), lambda i, k: (i, 0)),
        out_shape=jax.ShapeDtypeStruct((n_tiles * ROW_TILE, d), F32),
        scratch_shapes=[pltpu.VMEM((ROW_TILE, d), BF16)],
        compiler_params=_params("parallel", "arbitrary"),
        name="mlp_final" if final_norm else "mlp",
    )(h, gain, w_up, w_down, final_gain)


def _band_matrices(n, kp):
    r = lax.broadcasted_iota(jnp.int32, (n, kp), 0)
    c = lax.broadcasted_iota(jnp.int32, (n, kp), 1)
    off = c - r - HALO
    return [jnp.where((off >= -(w // 2)) & (off < w // 2), 1.0, 0.0).astype(BF16) for w in POOL_WINDOWS]


def _extended_rows(prev, cur, nxt, kp):
    pad = jnp.zeros((kp - cur.shape[0] - 2 * HALO, cur.shape[1]), F32)
    return jnp.concatenate([prev, cur, nxt, pad], axis=0).astype(BF16)


def _pool_rows(x, xn, xe_ref, band_ref, first_pos, seq_len, w_ref, scale_ref):
    n = x.shape[0]
    gdim = w_ref.shape[1]
    pos = first_pos + lax.broadcasted_iota(jnp.int32, (n, LANES), 0)
    outs = []
    for g, w in enumerate(POOL_WINDOWS):
        cols = slice(g * gdim, (g + 1) * gdim)
        window_sum = jnp.dot(band_ref[g], xe_ref[:, cols], preferred_element_type=F32)
        cnt = (jnp.minimum(pos + w // 2, seq_len) - jnp.maximum(pos - w // 2, 0)).astype(F32)
        inv = jnp.tile(1.0 / cnt, (1, gdim // LANES))
        mixed = (window_sum * inv - xn[:, cols]).astype(BF16)
        y = jnp.dot(mixed, w_ref[g], preferred_element_type=F32)
        outs.append(x[:, cols] + y * scale_ref[:, cols])
    return jnp.concatenate(outs, axis=-1)


def _pool_kernel(*refs, seq, n_real_tiles, with_tail):
    cur_ref, prev_ref, next_ref, gain_ref, w_ref, scale_ref = refs[:6]
    out_ref, xe_ref, band_ref = refs[-3:]
    i = pl.program_id(0)
    tiles = seq // ROW_TILE
    j = i % tiles

    @pl.when(i == 0)
    def _():
        for g, band in enumerate(_band_matrices(ROW_TILE, xe_ref.shape[0])):
            band_ref[g] = band

    @pl.when(i < n_real_tiles)
    def _():
        gain = gain_ref[...]
        x = cur_ref[...]
        xn = _rmsnorm(x, gain)
        nxt = jnp.where(j == tiles - 1, 0.0, _rmsnorm(next_ref[...], gain))
        xe_ref[...] = _extended_rows(_rmsnorm(prev_ref[...], gain), xn, nxt, xe_ref.shape[0])
        out_ref[...] = _pool_rows(x, xn, xe_ref, band_ref, N_META + j * ROW_TILE, N_META + seq, w_ref, scale_ref)

    if with_tail:
        @pl.when(i == n_real_tiles)
        def _():
            out_ref[...] = refs[6][...]


def _pool_meta_kernel(cur_ref, next_ref, gain_ref, w_ref, scale_ref, out_ref, xe_ref, band_ref, *, nseq, seq):
    b = pl.program_id(0)

    @pl.when(b == 0)
    def _():
        for g, band in enumerate(_band_matrices(N_META, xe_ref.shape[0])):
            band_ref[g] = band

    @pl.when(b < nseq)
    def _():
        gain = gain_ref[...]
        x = cur_ref[...]
        xn = _rmsnorm(x, gain)
        before = jnp.zeros((HALO, x.shape[1]), F32)
        xe_ref[...] = _extended_rows(before, xn, _rmsnorm(next_ref[...], gain), xe_ref.shape[0])
        out_ref[...] = _pool_rows(x, xn, xe_ref, band_ref, 0, N_META + seq, w_ref, scale_ref)

    @pl.when(b >= nseq)
    def _():
        out_ref[...] = jnp.zeros(out_ref.shape, F32)


def _pool_call(h, gain, w_pool, layer, scale, nseq, seq, n_real, with_meta):
    d = h.shape[1]
    tiles = seq // ROW_TILE
    n_real_tiles = n_real // ROW_TILE
    blk = ROW_TILE // HALO
    meta_blk0 = n_real // HALO
    w_spec = _layer_resident(w_pool, layer)
    in_specs, operands = [], []
    if with_meta:
        meta0 = n_real // N_META
        seq_of = lambda b: jnp.minimum(b, nseq - 1)
        tail = pl.pallas_call(
            functools.partial(_pool_meta_kernel, nseq=nseq, seq=seq),
            grid=(ROW_TILE // N_META,),
            in_specs=[pl.BlockSpec((N_META, d), lambda b: (meta0 + seq_of(b), 0)),
                      pl.BlockSpec((HALO, d), lambda b: (seq_of(b) * tiles * blk, 0)),
                      _vec_spec(d), w_spec, _vec_spec(d)],
            out_specs=pl.BlockSpec((N_META, d), lambda b: (b, 0)),
            out_shape=jax.ShapeDtypeStruct((ROW_TILE, d), F32),
            scratch_shapes=[pltpu.VMEM((LANES, d), BF16), pltpu.VMEM((len(POOL_WINDOWS), N_META, LANES), BF16)],
            compiler_params=_params("arbitrary"),
            name="pool_mixer_meta",
        )(h, h, gain, w_pool, scale)
        in_specs, operands = [pl.BlockSpec((ROW_TILE, d), lambda i: (0, 0))], [tail]

    real = lambda i: jnp.minimum(i, n_real_tiles - 1)
    cur_spec = pl.BlockSpec((ROW_TILE, d), lambda i: (real(i), 0))
    prev_spec = pl.BlockSpec((HALO, d), lambda i: (
        jnp.where(real(i) % tiles == 0, meta_blk0 + 2 * (real(i) // tiles) + 1, real(i) * blk - 1), 0))
    next_spec = pl.BlockSpec((HALO, d), lambda i: (jnp.where(real(i) % tiles == tiles - 1, 0, (real(i) + 1) * blk), 0))
    n_out_tiles = n_real_tiles + (1 if with_meta else 0)
    ext_rows = -(-(ROW_TILE + 2 * HALO) // LANES) * LANES
    return pl.pallas_call(
        functools.partial(_pool_kernel, seq=seq, n_real_tiles=n_real_tiles, with_tail=with_meta),
        grid=(n_out_tiles,),
        in_specs=[cur_spec, prev_spec, next_spec, _vec_spec(d), w_spec, _vec_spec(d)] + in_specs,
        out_specs=pl.BlockSpec((ROW_TILE, d), lambda i: (i, 0)),
        out_shape=jax.ShapeDtypeStruct((n_out_tiles * ROW_TILE, d), F32),
        scratch_shapes=[pltpu.VMEM((ext_rows, d), BF16), pltpu.VMEM((len(POOL_WINDOWS), ROW_TILE, ext_rows), BF16)],
        compiler_params=_params("arbitrary"),
        name="pool_mixer",
    )(h, h, h, gain, w_pool, scale, *operands)


def kernel(x_prompt, x_sample, meta_tokens, attn_norm, w_qkv, q_norm, k_norm, w_o, pool_norm, w_pool, pool_scale,
           mlp_norm, w_up, w_down, final_norm):
    b1, seq, d = x_prompt.shape
    b2 = x_sample.shape[0]
    assert x_sample.shape[1:] == (seq, d) and seq % ROW_TILE == 0 and seq % GRID_W == 0
    nseq = b1 + b2
    n_real = nseq * seq
    assert nseq * N_META <= ROW_TILE
    depth = mlp_norm.shape[0]
    t1, t2 = b1 * seq // ROW_TILE, b2 * seq // ROW_TILE

    tail = jnp.concatenate([jnp.tile(meta_tokens.astype(F32), (nseq, 1)),
                            jnp.zeros((ROW_TILE - nseq * N_META, d), F32)], axis=0)
    parts = [(x_prompt.reshape(b1 * seq, d), t1), (x_sample.reshape(b2 * seq, d), t2), (tail, 1)]

    nqk = (N_HEADS + N_KV_HEADS) * HEAD_DIM
    tables = _rope_tables(seq)
    w_qkv = jnp.concatenate([_split_pairs(w_qkv[..., :nqk], N_HEADS + N_KV_HEADS), w_qkv[..., nqk:]], axis=-1)
    w_qkv, w_o, w_pool, w_up, w_down = (w.astype(BF16) for w in (w_qkv, w_o, w_pool, w_up, w_down))
    row2 = lambda a: a.reshape(1, -1).astype(F32)
    q_gain = _split_pairs(q_norm.astype(F32), 1) * (HEAD_DIM ** -0.5 * math.log2(math.e))
    k_gain = _split_pairs(k_norm.astype(F32), 1)

    outs = None
    h = None
    for i in range(depth):
        j = i // 2
        last = i == depth - 1
        if i % 2 == 0:
            q, k, v = _qkv_call(parts, row2(attn_norm[j]), w_qkv, j, row2(q_gain[j]), row2(k_gain[j]), tables, seq, n_real)
            o_real, o_meta = _attn_call(q, k, v, nseq, seq, n_real)
            h = _oproj_call(parts, o_real, o_meta, w_o, j)
        else:
            h = _pool_call(h, row2(pool_norm[j]), w_pool, j, row2(pool_scale[j]), nseq, seq, n_real, with_meta=not last)
        if not last:
            h = _mlp_call(h, row2(mlp_norm[i]), w_up, w_down, i, row2(final_norm), 0, h.shape[0] // ROW_TILE, False)
            parts = [(h, h.shape[0] // ROW_TILE)]
        else:
            mlp = functools.partial(_mlp_call, h, row2(mlp_norm[i]), w_up, w_down, i, row2(final_norm))
            outs = (mlp(0, t1, True).reshape(b1, seq, d), mlp(t1, t2, True).reshape(b2, seq, d))
    return outs
```
